```python
import math
import jax
import jax.numpy as jnp
from jax import lax
import numpy as np

D_MODEL = 1024
BATCH = 2
SEQ = 8192
DEPTH = 4

GRID_W = 64
CTX_LEN = 256
N_MIXERS = 4
CTX_READING_MIXERS = (0, 1)
RMS_EPS = 1e-6
MLP_HIDDEN = 4 * D_MODEL
N_MOD = 6

SSD_EXPAND = 2
SSD_INNER = SSD_EXPAND * D_MODEL
SSD_HEADDIM = 64
SSD_HEADS = SSD_INNER // SSD_HEADDIM
SSD_GROUPS = 4
SSD_HPG = SSD_HEADS // SSD_GROUPS
SSD_STATE = 128
SSD_CONV = 3
SSD_CHUNK = 128
SSD_BC = 2 * 2 * SSD_GROUPS * SSD_STATE
SSD_CONV_CH = SSD_INNER + SSD_BC
SSD_IN = SSD_INNER + SSD_CONV_CH + 2 * SSD_HEADS

NA_HEADS = 16
NA_HEADDIM = D_MODEL // NA_HEADS
NA_WIN_ROWS = 8
NA_WIN_COLS = 16

SC_CONV = 3

FN_GROUPS = 8
FN_GROUP_CH = D_MODEL // FN_GROUPS

kernel_name = 'hybrid_interleaved_diffusion_trunk'


def rmsnorm(t, g):
    tf = t.astype(jnp.float32)
    tf = tf * lax.rsqrt(jnp.mean(tf * tf, axis=-1, keepdims=True) + RMS_EPS)
    return tf.astype(t.dtype) * g


def modulate(t, shift, scale):
    return t * (1 + scale) + shift


def dwconv_centred(u, w):
    k, ch = w.shape
    return lax.conv_general_dilated(u, w[:, None, :], window_strides=(1,), padding=[(k // 2, k // 2)],
                                    dimension_numbers=('NWC', 'WIO', 'NWC'), feature_group_count=ch)


def sq_relu_mlp(h, w1, w2):
    return jnp.square(jax.nn.relu(h @ w1)) @ w2


def finish_layer(t, y, m, g, w1, w2):
    t = t + m[2] * rmsnorm(y, g[1])
    h2 = modulate(rmsnorm(t, g[2]), m[3], m[4])
    return t + m[5] * rmsnorm(sq_relu_mlp(h2, w1, w2), g[3])


def ssd_scan(x, dt, a, bm, cm, state0):
    bsz, seq = x.shape[:2]
    nc = seq // SSD_CHUNK

    def chunkify(t):
        return jnp.moveaxis(t.reshape(bsz, nc, SSD_CHUNK, *t.shape[2:]), 1, 0)

    mask = jnp.tril(jnp.ones((SSD_CHUNK, SSD_CHUNK), bool))[:, :, None, None]

    def step(state, inp):
        xc, dtc, bc, cc = inp
        cum = jnp.cumsum(dtc * a, axis=1)
        seg = cum[:, :, None] - cum[:, None, :]
        decay = jnp.exp(jnp.where(mask, seg, -jnp.inf))
        cb = jnp.einsum('bign,bjgn->bijg', cc, bc)
        scores = cb[..., None] * decay * dtc[:, None]
        y = jnp.einsum('bijge,bjgep->bigep', scores, xc)
        y = y + jnp.einsum('bign,bgepn->bigep', cc, state) * jnp.exp(cum)[..., None]
        w_end = jnp.exp(cum[:, -1:] - cum) * dtc
        state = state * jnp.exp(cum[:, -1])[..., None, None] + jnp.einsum('bjgn,bjge,bjgep->bgepn', bc, w_end, xc)
        return state, y

    state, ys = lax.scan(step, state0, (chunkify(x), chunkify(dt), chunkify(bm), chunkify(cm)))
    return jnp.moveaxis(ys, 0, 1).reshape(x.shape), state


def ssd_mixer(h, w_in, conv_w, conv_b, dt_bias, a_log, d_skip, norm_g, state0):
    bsz, seq, _ = h.shape
    proj = h @ w_in
    z = proj[..., :SSD_INNER]
    xbc = jax.nn.silu(dwconv_centred(proj[..., SSD_INNER:SSD_INNER + SSD_CONV_CH], conv_w) + conv_b)
    dt_raw = proj[..., SSD_INNER + SSD_CONV_CH:].reshape(bsz, seq, 2, SSD_HEADS)
    xs = xbc[..., :SSD_INNER].reshape(bsz, seq, SSD_GROUPS, SSD_HPG, SSD_HEADDIM).astype(jnp.float32)
    bc = xbc[..., SSD_INNER:].reshape(bsz, seq, 2, 2, SSD_GROUPS, SSD_STATE).astype(jnp.float32)
    dt = jax.nn.softplus(dt_raw.astype(jnp.float32) + dt_bias.astype(jnp.float32))
    dt = dt.reshape(bsz, seq, 2, SSD_GROUPS, SSD_HPG)
    a = -jnp.exp(a_log.astype(jnp.float32)).reshape(2, SSD_GROUPS, SSD_HPG)
    rev = lambda t: jnp.flip(t, axis=1)
    y_f, s_f = ssd_scan(xs, dt[:, :, 0], a[0], bc[:, :, 0, 0], bc[:, :, 0, 1], state0[0])
    y_b, s_b = ssd_scan(rev(xs), rev(dt[:, :, 1]), a[1], rev(bc[:, :, 1, 0]), rev(bc[:, :, 1, 1]), state0[1])
    d_tot = d_skip.astype(jnp.float32).sum(0).reshape(SSD_GROUPS, SSD_HPG)
    y = y_f + rev(y_b) + d_tot[..., None] * xs
    y = y.reshape(bsz, seq, SSD_INNER).astype(h.dtype) * jax.nn.silu(z)
    return rmsnorm(y, norm_g), jnp.stack([s_f, s_b])


def split_qkv(h, w_qkv):
    bsz, seq, _ = h.shape
    qkv = (h @ w_qkv).reshape(bsz, seq, 3, NA_HEADS, NA_HEADDIM)
    return qkv[:, :, 0], qkv[:, :, 1], qkv[:, :, 2]


def context_attention(qc, kc, vc):
    bsz, n, _, _ = qc.shape
    s = jnp.einsum('bqhd,bkhd->bhqk', qc, kc) * NA_HEADDIM ** -0.5
    p = jax.nn.softmax(s.astype(jnp.float32), axis=-1).astype(vc.dtype)
    return jnp.einsum('bhqk,bkhd->bqhd', p, vc).reshape(bsz, n, D_MODEL)


def neighbourhood_attention(h, kc, vc, w_qkv, rpb):
    bsz, seq, _ = h.shape
    rows = seq // GRID_W
    wr = min(NA_WIN_ROWS, rows)
    wc = NA_WIN_COLS
    n_loc = wr * wc
    q, k, v = split_qkv(h, w_qkv)
    q, k, v = (t.reshape(bsz, rows, GRID_W, NA_HEADS, NA_HEADDIM) for t in (q, k, v))
    cols = jnp.arange(GRID_W)
    col_idx = jnp.clip(cols - wc // 2, 0, GRID_W - wc)[:, None] + jnp.arange(wc)
    rpb_col = jnp.take(rpb, col_idx - cols[:, None] + NA_WIN_COLS - 1, axis=2)
    scale = NA_HEADDIM ** -0.5

    def row_block(r):
        r0 = jnp.clip(r - wr // 2, 0, rows - wr)
        q_r = lax.dynamic_index_in_dim(q, r, axis=1, keepdims=False)
        k_w = jnp.take(lax.dynamic_slice_in_dim(k, r0, wr, axis=1), col_idx, axis=2)
        v_w = jnp.take(lax.dynamic_slice_in_dim(v, r0, wr, axis=1), col_idx, axis=2)
        bias = jnp.take(rpb_col, r0 + jnp.arange(wr) - r + NA_WIN_ROWS - 1, axis=1)
        s_loc = jnp.einsum('bqhd,bwqchd->bqhwc', q_r, k_w) * scale + jnp.transpose(bias, (2, 0, 1, 3))
        s_ctx = jnp.einsum('bqhd,bkhd->bqhk', q_r, kc) * scale
        s = jnp.concatenate([s_loc.reshape(bsz, GRID_W, NA_HEADS, n_loc), s_ctx], axis=-1)
        p = jax.nn.softmax(s.astype(jnp.float32), axis=-1).astype(v.dtype)
        p_loc = p[..., :n_loc].reshape(bsz, GRID_W, NA_HEADS, wr, wc)
        return (jnp.einsum('bqhwc,bwqchd->bqhd', p_loc, v_w)
                + jnp.einsum('bqhk,bkhd->bqhd', p[..., n_loc:], vc))

    out = lax.map(row_block, jnp.arange(rows))
    return jnp.transpose(out, (1, 0, 2, 3, 4)).reshape(bsz, seq, D_MODEL)


def short_conv_mixer(h, w_in, conv_w, w_out):
    b_gate, c_gate, u = jnp.split(h @ w_in, 3, axis=-1)
    return (b_gate * dwconv_centred(c_gate * u, conv_w)) @ w_out


def fourier_mixer(h, w_out):
    bsz, seq, _ = h.shape
    hg = h.astype(jnp.float32).reshape(bsz, seq, FN_GROUPS, FN_GROUP_CH)
    f = jnp.fft.fft2(hg, axes=(1, 3)).real.reshape(bsz, seq, D_MODEL)
    return f.astype(h.dtype) @ w_out


def setup_inputs(seed: int = 0) -> dict:
    key = jax.random.key(seed)
    kit = iter(list(jax.random.split(key, 32)))
    f32 = jnp.float32

    def nrm(shape, scale):
        return jax.random.normal(next(kit), shape, f32) * scale

    n_a, n_b, n_c, n_d = (len(range(k, DEPTH, N_MIXERS)) for k in range(N_MIXERS))
    dt0 = jnp.exp(jax.random.uniform(next(kit), (n_a, 2, SSD_HEADS), f32, math.log(1e-3), math.log(1e-1)))
    return {
        'x': nrm((BATCH, SEQ, D_MODEL), 1.0),
        'c': nrm((BATCH, D_MODEL), 1.0),
        'ctx': nrm((BATCH, CTX_LEN, D_MODEL), 1.0),
        'c_ctx': nrm((D_MODEL,), 1.0),
        'mod_w': nrm((DEPTH, D_MODEL, N_MOD * D_MODEL), 0.5 * D_MODEL ** -0.5),
        'mod_b': nrm((DEPTH, N_MOD * D_MODEL), 0.02),
        'norm_g': 1.0 + nrm((DEPTH, 4, D_MODEL), 0.05),
        'mlp_w1': nrm((DEPTH, D_MODEL, MLP_HIDDEN), D_MODEL ** -0.5),
        'mlp_w2': nrm((DEPTH, MLP_HIDDEN, D_MODEL), MLP_HIDDEN ** -0.5),
        'ssd_w_in': nrm((n_a, D_MODEL, SSD_IN), D_MODEL ** -0.5),
        'ssd_conv_w': nrm((n_a, SSD_CONV, SSD_CONV_CH), SSD_CONV ** -0.5),
        'ssd_conv_b': nrm((n_a, SSD_CONV_CH), 0.02),
        'ssd_dt_bias': dt0 + jnp.log(-jnp.expm1(-dt0)),
        'ssd_a_log': jnp.log(jax.random.uniform(next(kit), (n_a, 2, SSD_HEADS), f32, 1.0, 16.0)),
        'ssd_d': 1.0 + nrm((n_a, 2, SSD_HEADS), 0.1),
        'ssd_norm_g': 1.0 + nrm((n_a, SSD_INNER), 0.05),
        'ssd_w_out': nrm((n_a, SSD_INNER, D_MODEL), SSD_INNER ** -0.5),
        'na_w_qkv': nrm((n_b, D_MODEL, 3 * D_MODEL), D_MODEL ** -0.5),
        'na_rpb': nrm((n_b, NA_HEADS, 2 * NA_WIN_ROWS - 1, 2 * NA_WIN_COLS - 1), 0.1),
        'na_w_out': nrm((n_b, D_MODEL, D_MODEL), D_MODEL ** -0.5),
        'sc_w_in': nrm((n_c, D_MODEL, 3 * D_MODEL), D_MODEL ** -0.5),
        'sc_conv_w': nrm((n_c, SC_CONV, D_MODEL), SC_CONV ** -0.5),
        'sc_w_out': nrm((n_c, D_MODEL, D_MODEL), D_MODEL ** -0.5),
        'fn_w_out': nrm((n_d, D_MODEL, D_MODEL), D_MODEL ** -0.5),
    }


def reference(x, c, ctx, c_ctx, mod_w, mod_b, norm_g, mlp_w1, mlp_w2, ssd_w_in, ssd_conv_w, ssd_conv_b,
              ssd_dt_bias, ssd_a_log, ssd_d, ssd_norm_g, ssd_w_out, na_w_qkv, na_rpb, na_w_out,
              sc_w_in, sc_conv_w, sc_w_out, fn_w_out):
    c_act = jax.nn.silu(c)
    c_ctx_act = jax.nn.silu(c_ctx)
    for i in range(DEPTH):
        kind, j = i % N_MIXERS, i // N_MIXERS
        ctx_needed_later = any((l % N_MIXERS) in CTX_READING_MIXERS for l in range(i + 1, DEPTH))
        m_lat = [m[:, None, :] for m in jnp.split(c_act @ mod_w[i] + mod_b[i], N_MOD, axis=-1)]
        m_ctx = jnp.split(c_ctx_act @ mod_w[i] + mod_b[i], N_MOD, axis=-1)
        g = norm_g[i]
        h = modulate(rmsnorm(x, g[0]), m_lat[0], m_lat[1])
        hc = None
        if kind in CTX_READING_MIXERS or ctx_needed_later:
            hc = modulate(rmsnorm(ctx, g[0]), m_ctx[0], m_ctx[1])
        y_ctx = None
        if kind == 0:
            prm = (ssd_w_in[j], ssd_conv_w[j], ssd_conv_b[j], ssd_dt_bias[j], ssd_a_log[j], ssd_d[j], ssd_norm_g[j])
            zero_state = jnp.zeros((2, hc.shape[0], SSD_GROUPS, SSD_HPG, SSD_HEADDIM, SSD_STATE), jnp.float32)
            yc_inner, ctx_states = ssd_mixer(hc, *prm, zero_state)
            y_inner, _ = ssd_mixer(h, *prm, ctx_states)
            y = y_inner @ ssd_w_out[j]
            if ctx_needed_later:
                y_ctx = yc_inner @ ssd_w_out[j]
        elif kind == 1:
            qc, kc, vc = split_qkv(hc, na_w_qkv[j])
            y = neighbourhood_attention(h, kc, vc, na_w_qkv[j], na_rpb[j]) @ na_w_out[j]
            if ctx_needed_later:
                y_ctx = context_attention(qc, kc, vc) @ na_w_out[j]
        elif kind == 2:
            y = short_conv_mixer(h, sc_w_in[j], sc_conv_w[j], sc_w_out[j])
            if ctx_needed_later:
                y_ctx = short_conv_mixer(hc, sc_w_in[j], sc_conv_w[j], sc_w_out[j])
        else:
            y = fourier_mixer(h, fn_w_out[j])
            if ctx_needed_later:
                y_ctx = fourier_mixer(hc, fn_w_out[j])
        x = finish_layer(x, y, m_lat, g, mlp_w1[i], mlp_w2[i])
        if ctx_needed_later:
            ctx = finish_layer(ctx, y_ctx, m_ctx, g, mlp_w1[i], mlp_w2[i])
    return x
```

```python
import functools
import math

import jax
import jax.numpy as jnp
import numpy as np
from jax import lax
from jax.experimental import pallas as pl
from jax.experimental.pallas import tpu as pltpu

F32 = jnp.float32
BF16 = jnp.bfloat16
HIGHEST = lax.Precision.HIGHEST

RMS_EPS = 1e-6
N_MOD = 6
GRID_W = 64
HALO = 8
SSD_CHUNK = 128
SSD_HEADDIM = 64
SSD_STATE = 128
SSD_GROUPS = 4
NA_HEADS = 16
NA_HEADDIM = 64
NA_WIN_ROWS = 8
NA_WIN_COLS = 16
FN_GROUPS = 8
DFT_ROWS = 64
NEG_BIG = -1e30
VMEM_LIMIT = 56 * 1024 * 1024


def _cparams(*sem):
    return pltpu.CompilerParams(dimension_semantics=sem, vmem_limit_bytes=VMEM_LIMIT)


def _full_spec(shape):
    nd = len(shape)
    return pl.BlockSpec(shape, lambda *_: (0,) * nd)


def _rms(t, g):
    return t * lax.rsqrt(jnp.mean(t * t, axis=-1, keepdims=True) + RMS_EPS) * g


def _silu(t):
    return t * jax.nn.sigmoid(t)


def _softplus(t):
    return jnp.maximum(t, 0.0) + jnp.log1p(jnp.exp(-jnp.abs(t)))


def _dot(a, b, **kw):
    return jnp.dot(a, b, preferred_element_type=F32, **kw)


def _dot_nt(a, b, **kw):
    return lax.dot_general(a, b, (((1,), (1,)), ((), ())), preferred_element_type=F32, **kw)


def _mod_kernel(c_ref, w_ref, b_ref, o_ref):
    act = _silu(c_ref[...])
    o_ref[0] = _dot(act, w_ref[0], precision=HIGHEST) + b_ref[0]


def _modulation(c_rows, mod_w, mod_b):
    depth, d, n = mod_w.shape
    tn = n // 4
    rows = c_rows.shape[0]
    return pl.pallas_call(
        _mod_kernel,
        grid=(depth, n // tn),
        in_specs=[
            pl.BlockSpec((rows, d), lambda i, j: (0, 0)),
            pl.BlockSpec((1, d, tn), lambda i, j: (i, 0, j)),
            pl.BlockSpec((1, 1, tn), lambda i, j: (i, 0, j)),
        ],
        out_specs=pl.BlockSpec((1, rows, tn), lambda i, j: (i, 0, j)),
        out_shape=jax.ShapeDtypeStruct((depth, rows, n), F32),
        compiler_params=_cparams("parallel", "parallel"),
        name="modulation",
    )(c_rows, mod_w, mod_b.reshape(depth, 1, n))


def _row_specs(tm, d, tiles_per_group):
    x_spec = pl.BlockSpec((tm, d), lambda i: (i, 0))
    m_spec = pl.BlockSpec((1, 1, d), lambda i: (i // tiles_per_group, 0, 0))
    return x_spec, m_spec


def _halo_specs(tm, d, n_rows):
    per = tm // HALO
    last = n_rows // HALO - 1
    prev = pl.BlockSpec((HALO, d), lambda i: (jnp.maximum(i * per - 1, 0), 0))
    nxt = pl.BlockSpec((HALO, d), lambda i: (jnp.minimum((i + 1) * per, last), 0))
    return prev, nxt


def _normed_with_halo(xp_ref, x_ref, xn_ref, g_ref, sh_ref, sc_ref):
    x_all = jnp.concatenate([xp_ref[...], x_ref[...], xn_ref[...]], axis=0)
    return _rms(x_all, g_ref[...]) * (1.0 + sc_ref[0]) + sh_ref[0]


def _halo_row_mask(tm, tiles_per_seq):
    i = pl.program_id(0)
    row = lax.broadcasted_iota(jnp.int32, (tm + 2 * HALO, 1), 0)
    first = (i % tiles_per_seq) == 0
    last = (i % tiles_per_seq) == tiles_per_seq - 1
    dead = jnp.logical_or(jnp.logical_and(first, row < HALO), jnp.logical_and(last, row >= tm + HALO))
    return jnp.where(dead, 0.0, 1.0)


def _conv3(v, w_ref, tm):
    n = v.shape[0]
    prev = pltpu.roll(v, 1, 0)[HALO:HALO + tm]
    cur = v[HALO:HALO + tm]
    nxt = pltpu.roll(v, n - 1, 0)[HALO:HALO + tm]
    return prev * w_ref[0:1, :] + cur * w_ref[1:2, :] + nxt * w_ref[2:3, :]


def _ssd_inproj_kernel(xp_ref, x_ref, xn_ref, g_ref, sh_ref, sc_ref, wz_ref, wx_ref, wdt_ref, wdtt_ref,
                       cw_ref, cb_ref, dtb_ref, dtbt_ref,
                       z_ref, xs_ref, bcf_ref, bcb_ref, dt_ref, dtt_ref, *, tm, tiles_per_seq, inner):
    h_all = _normed_with_halo(xp_ref, x_ref, xn_ref, g_ref, sh_ref, sc_ref)
    hb_all = h_all.astype(BF16)
    hb = h_all[HALO:HALO + tm].astype(BF16)
    mask = _halo_row_mask(tm, tiles_per_seq)

    z_ref[...] = _dot(hb, wz_ref[...])
    dt_ref[...] = _softplus(_dot(hb, wdt_ref[...]) + dtb_ref[...])
    dtt_ref[...] = _softplus(_dot_nt(wdtt_ref[...], hb) + dtbt_ref[...])

    cw = 1024
    n_ch = wx_ref.shape[1]
    for n0 in range(0, n_ch, cw):
        raw = _dot(hb_all, wx_ref[:, n0:n0 + cw]) * mask
        act = _silu(_conv3(raw, cw_ref.at[:, n0:n0 + cw], tm) + cb_ref[:, n0:n0 + cw])
        if n0 < inner:
            xs_ref[:, n0:n0 + cw] = act
        elif n0 < inner + cw:
            bcf_ref[...] = act.astype(BF16)
        else:
            bcb_ref[...] = act.astype(BF16)


def _ssd_inproj(x2, n_seq_rows, g0, shift, scale, tiles_per_group_rows, prm, tm):
    n_rows, d = x2.shape
    wz, wx, wdt, wdtt, conv_w, conv_b, dtb, dtbt = prm
    inner = wz.shape[1]
    nbc = (wx.shape[1] - inner) // 2
    nh2 = wdt.shape[1]
    x_spec, m_spec = _row_specs(tm, d, tiles_per_group_rows // tm)
    prev, nxt = _halo_specs(tm, d, n_rows)
    kern = functools.partial(_ssd_inproj_kernel, tm=tm, tiles_per_seq=n_seq_rows // tm, inner=inner)
    return pl.pallas_call(
        kern,
        grid=(n_rows // tm,),
        in_specs=[prev, x_spec, nxt, _full_spec((1, d)), m_spec, m_spec,
                  _full_spec(wz.shape), _full_spec(wx.shape), _full_spec(wdt.shape), _full_spec(wdtt.shape),
                  _full_spec(conv_w.shape), _full_spec(conv_b.shape), _full_spec(dtb.shape), _full_spec(dtbt.shape)],
        out_specs=[pl.BlockSpec((tm, inner), lambda i: (i, 0)),
                   pl.BlockSpec((tm, inner), lambda i: (i, 0)),
                   pl.BlockSpec((tm, nbc), lambda i: (i, 0)),
                   pl.BlockSpec((tm, nbc), lambda i: (i, 0)),
                   pl.BlockSpec((tm, nh2), lambda i: (i, 0)),
                   pl.BlockSpec((nh2, tm), lambda i: (0, i))],
        out_shape=[jax.ShapeDtypeStruct((n_rows, inner), F32),
                   jax.ShapeDtypeStruct((n_rows, inner), F32),
                   jax.ShapeDtypeStruct((n_rows, nbc), BF16),
                   jax.ShapeDtypeStruct((n_rows, nbc), BF16),
                   jax.ShapeDtypeStruct((n_rows, nh2), F32),
                   jax.ShapeDtypeStruct((nh2, n_rows), F32)],
        compiler_params=_cparams("parallel"),
        name="ssd_inproj",
    )(x2, x2, x2, g0, shift, scale, wz, wx, wdt, wdtt, conv_w, conv_b, dtb, dtbt)


def _ssd_chunk(d, r0, x_ref, bc_ref, dt_ref, dtt_ref, a_row, a_col, s_ref, y_ref, tri, mask, lane_lo):
    nh = a_row.shape[1] // 2
    hpg = nh // SSD_GROUPS
    rows = pl.ds(r0, SSD_CHUNK)
    dt = dt_ref[rows, :]
    dtt = dtt_ref[:, rows]
    cum = _dot(tri, dt * a_row, precision=HIGHEST)
    cumt = _dot_nt(dtt * a_col, tri, precision=HIGHEST)
    end = SSD_CHUNK - 1 if d == 0 else 0
    tot = cum[end:end + 1, :]
    tott = cumt[:, end:end + 1]
    wt = jnp.exp(tott - cumt) * dtt
    dec_s = jnp.exp(tot)
    for g in range(SSD_GROUPS):
        bg = bc_ref[rows, g * SSD_STATE:(g + 1) * SSD_STATE]
        cg = bc_ref[rows, (SSD_GROUPS + g) * SSD_STATE:(SSD_GROUPS + g + 1) * SSD_STATE]
        cb = _dot_nt(cg, bg)
        bt = bg.astype(F32).T
        cgf = cg.astype(F32)
        for p in range(hpg // 2):
            l0 = (g * hpg + 2 * p) * SSD_HEADDIM
            x2 = x_ref[rows, l0:l0 + 2 * SSD_HEADDIM]
            x2b = x2.astype(BF16)
            s2 = s_ref[0, d, :, l0:l0 + 2 * SSD_HEADDIM]
            rhs = jnp.concatenate([x2b, s2.astype(BF16)], axis=0)
            outs, ups, decs = [], [], []
            for hx in range(2):
                col = d * nh + g * hpg + 2 * p + hx
                colb = cum[:, col:col + 1]
                rowb = cumt[col:col + 1, :]
                dec = jnp.exp(jnp.where(mask, colb - rowb, NEG_BIG))
                sc = (cb * dec * dtt[col:col + 1, :]).astype(BF16)
                co = (cgf * jnp.exp(colb)).astype(BF16)
                outs.append(_dot(jnp.concatenate([sc, co], axis=1), rhs))
                ups.append(_dot((bt * wt[col:col + 1, :]).astype(BF16), x2b))
                decs.append(dec_s[:, col:col + 1])
            y_ref[rows, l0:l0 + 2 * SSD_HEADDIM] = jnp.where(lane_lo, outs[0], outs[1])
            s_ref[0, d, :, l0:l0 + 2 * SSD_HEADDIM] = (
                s2 * jnp.where(lane_lo, decs[0], decs[1]) + jnp.where(lane_lo, ups[0], ups[1]))


def _ssd_scan_kernel(xf_ref, xb_ref, bcf_ref, bcb_ref, dtf_ref, dtb_ref, dttf_ref, dttb_ref,
                     al_ref, alt_ref, s0_ref, yf_ref, yb_ref, s_ref, *, n_chunk):
    @pl.when(pl.program_id(1) == 0)
    def _():
        s_ref[...] = s0_ref[...]

    a_row = -jnp.exp(al_ref[...])
    a_col = -jnp.exp(alt_ref[...])
    ii = lax.broadcasted_iota(jnp.int32, (SSD_CHUNK, SSD_CHUNK), 0)
    jj = lax.broadcasted_iota(jnp.int32, (SSD_CHUNK, SSD_CHUNK), 1)
    lane_lo = lax.broadcasted_iota(jnp.int32, (1, 2 * SSD_HEADDIM), 1) < SSD_HEADDIM
    for c in range(n_chunk):
        low = jj <= ii
        _ssd_chunk(0, c * SSD_CHUNK, xf_ref, bcf_ref, dtf_ref, dttf_ref, a_row, a_col, s_ref, yf_ref,
                   jnp.where(low, 1.0, 0.0), low, lane_lo)
        up = jj >= ii
        _ssd_chunk(1, (n_chunk - 1 - c) * SSD_CHUNK, xb_ref, bcb_ref, dtb_ref, dttb_ref, a_row, a_col, s_ref,
                   yb_ref, jnp.where(up, 1.0, 0.0), up, lane_lo)


def _ssd_scan(xs, bcf, bcb, dt, dtt, a_log_row, a_log_col, state0, batch, tb):
    n_rows, inner = xs.shape
    nbc = bcf.shape[1]
    nh2 = dt.shape[1]
    nb = n_rows // batch // tb
    fwd = lambda b, k: (b * nb + k, 0)
    bwd = lambda b, k: (b * nb + nb - 1 - k, 0)
    fwd_t = lambda b, k: (0, b * nb + k)
    bwd_t = lambda b, k: (0, b * nb + nb - 1 - k)
    s_spec = pl.BlockSpec((1, 2, SSD_STATE, inner), lambda b, k: (b, 0, 0, 0))
    kern = functools.partial(_ssd_scan_kernel, n_chunk=tb // SSD_CHUNK)
    return pl.pallas_call(
        kern,
        grid=(batch, nb),
        in_specs=[pl.BlockSpec((tb, inner), fwd), pl.BlockSpec((tb, inner), bwd),
                  pl.BlockSpec((tb, nbc), fwd), pl.BlockSpec((tb, nbc), bwd),
                  pl.BlockSpec((tb, nh2), fwd), pl.BlockSpec((tb, nh2), bwd),
                  pl.BlockSpec((nh2, tb), fwd_t), pl.BlockSpec((nh2, tb), bwd_t),
                  _full_spec(a_log_row.shape), _full_spec(a_log_col.shape), s_spec],
        out_specs=[pl.BlockSpec((tb, inner), fwd), pl.BlockSpec((tb, inner), bwd), s_spec],
        out_shape=[jax.ShapeDtypeStruct((n_rows, inner), F32),
                   jax.ShapeDtypeStruct((n_rows, inner), F32),
                   jax.ShapeDtypeStruct(state0.shape, F32)],
        compiler_params=_cparams("parallel", "arbitrary"),
        name="ssd_scan",
    )(xs, xs, bcf, bcb, dt, dt, dtt, dtt, a_log_row, a_log_col, state0)


def _ssd_gate_kernel(yf_ref, yb_ref, xs_ref, z_ref, d_ref, g_ref, o_ref):
    y = yf_ref[...] + yb_ref[...] + d_ref[...] * xs_ref[...]
    y = y * _silu(z_ref[...])
    o_ref[...] = _rms(y, g_ref[...]).astype(BF16)


def _ssd_gate(yf, yb, xs, z, d_vec, norm_g, tm):
    n_rows, inner = yf.shape
    spec = pl.BlockSpec((tm, inner), lambda i: (i, 0))
    return pl.pallas_call(
        _ssd_gate_kernel,
        grid=(n_rows // tm,),
        in_specs=[spec, spec, spec, spec, _full_spec((1, inner)), _full_spec((1, inner))],
        out_specs=spec,
        out_shape=jax.ShapeDtypeStruct((n_rows, inner), BF16),
        compiler_params=_cparams("parallel"),
        name="ssd_gate",
    )(yf, yb, xs, z, d_vec, norm_g)


def _qkv_kernel(x_ref, g_ref, sh_ref, sc_ref, w_ref, q_ref, k_ref, v_ref):
    d = x_ref.shape[1]
    hb = (_rms(x_ref[...], g_ref[...]) * (1.0 + sc_ref[0]) + sh_ref[0]).astype(BF16)
    q_ref[...] = (_dot(hb, w_ref[:, 0:d]) * (NA_HEADDIM ** -0.5)).astype(BF16)
    k_ref[...] = _dot(hb, w_ref[:, d:2 * d]).astype(BF16)
    v_ref[...] = _dot(hb, w_ref[:, 2 * d:3 * d]).astype(BF16)


def _qkv(x2, g0, shift, scale, group_rows, w, tm):
    n_rows, d = x2.shape
    x_spec, m_spec = _row_specs(tm, d, group_rows // tm)
    out = jax.ShapeDtypeStruct((n_rows, d), BF16)
    return pl.pallas_call(
        _qkv_kernel,
        grid=(n_rows // tm,),
        in_specs=[x_spec, _full_spec((1, d)), m_spec, m_spec, _full_spec(w.shape)],
        out_specs=[x_spec, x_spec, x_spec],
        out_shape=[out, out, out],
        compiler_params=_cparams("parallel"),
        name="na_qkv",
    )(x2, g0, shift, scale, w)


def _na_kernel(*refs):
    q_ref = refs[0]
    k_refs = refs[1:1 + NA_WIN_ROWS]
    v_refs = refs[1 + NA_WIN_ROWS:1 + 2 * NA_WIN_ROWS]
    kc_ref, vc_ref, bias_ref, o_ref = refs[1 + 2 * NA_WIN_ROWS:]
    lane_lo = lax.broadcasted_iota(jnp.int32, (1, 2 * NA_HEADDIM), 1) < NA_HEADDIM
    for p in range(NA_HEADS // 2):
        lanes = slice(p * 2 * NA_HEADDIM, (p + 1) * 2 * NA_HEADDIM)
        q2 = q_ref[:, lanes]
        k2 = jnp.concatenate([r[:, lanes] for r in k_refs], axis=0)
        v2 = jnp.concatenate([r[:, lanes] for r in v_refs], axis=0)
        kc2 = kc_ref[0, :, lanes]
        vc2 = vc_ref[0, :, lanes]
        outs = []
        for hx in range(2):
            keep = lane_lo if hx == 0 else jnp.logical_not(lane_lo)
            qh = jnp.where(keep, q2, jnp.zeros_like(q2))
            s_loc = _dot_nt(qh, k2) + bias_ref[0, 2 * p + hx]
            s_ctx = _dot_nt(qh, kc2)
            m = jnp.maximum(jnp.max(s_loc, axis=-1, keepdims=True), jnp.max(s_ctx, axis=-1, keepdims=True))
            p_loc = jnp.exp(s_loc - m)
            p_ctx = jnp.exp(s_ctx - m)
            den = jnp.sum(p_loc, axis=-1, keepdims=True) + jnp.sum(p_ctx, axis=-1, keepdims=True)
            o = _dot(p_loc.astype(BF16), v2) + _dot(p_ctx.astype(BF16), vc2)
            outs.append(o / den)
        o_ref[:, lanes] = jnp.where(lane_lo, outs[0], outs[1]).astype(BF16)


def _na_bias_table(rpb):
    cols = np.arange(GRID_W)
    c0 = np.clip(cols - NA_WIN_COLS // 2, 0, GRID_W - NA_WIN_COLS)
    kc = np.arange(GRID_W)
    valid = (kc[None, :] >= c0[:, None]) & (kc[None, :] < c0[:, None] + NA_WIN_COLS)
    co = np.clip(kc[None, :] - cols[:, None] + NA_WIN_COLS - 1, 0, 2 * NA_WIN_COLS - 2)
    w = np.arange(NA_WIN_ROWS)
    ro = w[None, :] - np.arange(NA_WIN_ROWS)[:, None] + NA_WIN_ROWS - 1
    tab = rpb[:, ro[:, :, None, None], co[None, None, :, :]]
    tab = jnp.where(valid[None, None, None], tab, NEG_BIG)
    tab = jnp.transpose(tab, (1, 0, 3, 2, 4))
    return tab.reshape(NA_WIN_ROWS, rpb.shape[0], GRID_W, NA_WIN_ROWS * GRID_W)


def _na_attention(q, k, v, kc, vc, bias_tab, batch):
    n_rows, d = q.shape
    rows = n_rows // batch // GRID_W
    wr = NA_WIN_ROWS
    start = lambda r: jnp.clip(r - wr // 2, 0, rows - wr)
    q_spec = pl.BlockSpec((GRID_W, d), lambda b, r: (b * rows + r, 0))
    win = [pl.BlockSpec((GRID_W, d), functools.partial(lambda b, r, w: (b * rows + start(r) + w, 0), w=w))
           for w in range(wr)]
    c_spec = pl.BlockSpec((1,) + kc.shape[1:], lambda b, r: (b, 0, 0))
    bias_spec = pl.BlockSpec((1,) + bias_tab.shape[1:], lambda b, r: (r - start(r), 0, 0, 0))
    return pl.pallas_call(
        _na_kernel,
        grid=(batch, rows),
        in_specs=[q_spec] + win + win + [c_spec, c_spec, bias_spec],
        out_specs=q_spec,
        out_shape=jax.ShapeDtypeStruct((n_rows, d), BF16),
        compiler_params=_cparams("parallel", "arbitrary"),
        name="na_attention",
    )(q, *([k] * wr), *([v] * wr), kc, vc, bias_tab)


def _sconv_kernel(xp_ref, x_ref, xn_ref, g_ref, sh_ref, sc_ref, w_ref, cw_ref, o_ref, *, tm, tiles_per_seq):
    d = x_ref.shape[1]
    h_all = _normed_with_halo(xp_ref, x_ref, xn_ref, g_ref, sh_ref, sc_ref)
    hb_all = h_all.astype(BF16)
    hb = h_all[HALO:HALO + tm].astype(BF16)
    mask = _halo_row_mask(tm, tiles_per_seq)
    cw = 512
    for n0 in range(0, d, cw):
        c_gate = _dot(hb_all, w_ref[:, d + n0:d + n0 + cw])
        u = _dot(hb_all, w_ref[:, 2 * d + n0:2 * d + n0 + cw])
        conv = _conv3(c_gate * u * mask, cw_ref.at[:, n0:n0 + cw], tm)
        b_gate = _dot(hb, w_ref[:, n0:n0 + cw])
        o_ref[:, n0:n0 + cw] = (b_gate * conv).astype(BF16)


def _sconv(x2, seq_rows, g0, shift, scale, w, conv_w, tm):
    n_rows, d = x2.shape
    x_spec, m_spec = _row_specs(tm, d, seq_rows // tm)
    prev, nxt = _halo_specs(tm, d, n_rows)
    kern = functools.partial(_sconv_kernel, tm=tm, tiles_per_seq=seq_rows // tm)
    return pl.pallas_call(
        kern,
        grid=(n_rows // tm,),
        in_specs=[prev, x_spec, nxt, _full_spec((1, d)), m_spec, m_spec, _full_spec(w.shape),
                  _full_spec(conv_w.shape)],
        out_specs=x_spec,
        out_shape=jax.ShapeDtypeStruct((n_rows, d), BF16),
        compiler_params=_cparams("parallel"),
        name="sconv",
    )(x2, x2, x2, g0, shift, scale, w, conv_w)


def _fourier1_kernel(x_ref, g_ref, sh_ref, sc_ref, cs_ref, m_ref, twc_ref, tws_ref, yr_ref, yi_ref, *, nt2):
    d = g_ref.shape[1]
    gc = d // FN_GROUPS
    for j in range(nt2):
        xs = x_ref[0, :, j * d:(j + 1) * d]
        h = _rms(xs, g_ref[...]) * (1.0 + sc_ref[0]) + sh_ref[0]
        zr, zi = [], []
        for g in range(FN_GROUPS):
            zz = _dot(h[:, g * gc:(g + 1) * gc], cs_ref[...], precision=HIGHEST)
            zr.append(zz[:, :gc])
            zi.append(zz[:, gc:])
        z = jnp.concatenate([jnp.concatenate(zr, axis=1), jnp.concatenate(zi, axis=1)], axis=0)
        y = _dot(m_ref[...], z, precision=HIGHEST)
        yr, yi = y[:DFT_ROWS], y[DFT_ROWS:]
        tc, ts = twc_ref[j], tws_ref[j]
        yr_ref[0, j] = yr * tc + yi * ts
        yi_ref[0, j] = yi * tc - yr * ts


def _fourier2_kernel(yr_ref, yi_ref, gc_ref, gs_ref, o_ref):
    o_ref[0] = (_dot(gc_ref[...], yr_ref[0], precision=HIGHEST)
                + _dot(gs_ref[...], yi_ref[0], precision=HIGHEST)).astype(o_ref.dtype)


def _dft_tables(seq, gc):
    t2n = seq // DFT_ROWS
    c = np.arange(gc)
    ang_c = 2.0 * np.pi * np.outer(c, c) / gc
    cs = np.concatenate([np.cos(ang_c), -np.sin(ang_c)], axis=1)
    k1 = np.arange(DFT_ROWS)
    ang1 = 2.0 * np.pi * np.outer(k1, k1) / DFT_ROWS
    c1, s1 = np.cos(ang1), np.sin(ang1)
    m = np.block([[c1, s1], [-s1, c1]])
    t2 = np.arange(t2n)
    ang_tw = 2.0 * np.pi * np.outer(t2, k1) / seq
    ang2 = 2.0 * np.pi * np.outer(t2, t2) / t2n
    f = lambda a: jnp.asarray(a, F32)
    return (f(cs), f(m), f(np.cos(ang_tw))[:, :, None], f(np.sin(ang_tw))[:, :, None],
            f(np.cos(ang2)), f(np.sin(ang2)))


def _fourier(x3, g0, shift, scale):
    batch, seq, d = x3.shape
    t2n = seq // DFT_ROWS
    nt2 = min(8, t2n)
    cs, m, twc, tws, g2c, g2s = _dft_tables(seq, d // FN_GROUPS)
    xv = x3.reshape(batch, DFT_ROWS, t2n * d)
    m_spec = pl.BlockSpec((1, 1, d), lambda b, j: (b, 0, 0))
    y_spec = pl.BlockSpec((1, nt2, DFT_ROWS, d), lambda b, j: (b, j, 0, 0))
    tw_spec = pl.BlockSpec((nt2, DFT_ROWS, 1), lambda b, j: (j, 0, 0))
    y_shape = jax.ShapeDtypeStruct((batch, t2n, DFT_ROWS, d), F32)
    yr, yi = pl.pallas_call(
        functools.partial(_fourier1_kernel, nt2=nt2),
        grid=(batch, t2n // nt2),
        in_specs=[pl.BlockSpec((1, DFT_ROWS, nt2 * d), lambda b, j: (b, 0, j)),
                  _full_spec((1, d)), m_spec, m_spec, _full_spec(cs.shape), _full_spec(m.shape), tw_spec, tw_spec],
        out_specs=[y_spec, y_spec],
        out_shape=[y_shape, y_shape],
        compiler_params=_cparams("parallel", "parallel"),
        name="fourier_stage1",
    )(xv, g0, shift, scale, cs, m, twc, tws)
    lanes = DFT_ROWS * d
    tl = min(4096, lanes)
    yv_spec = pl.BlockSpec((1, t2n, tl), lambda b, j: (b, 0, j))
    out = pl.pallas_call(
        _fourier2_kernel,
        grid=(batch, lanes // tl),
        in_specs=[yv_spec, yv_spec, _full_spec(g2c.shape), _full_spec(g2s.shape)],
        out_specs=yv_spec,
        out_shape=jax.ShapeDtypeStruct((batch, t2n, lanes), BF16),
        compiler_params=_cparams("parallel", "parallel"),
        name="fourier_stage2",
    )(yr.reshape(batch, t2n, lanes), yi.reshape(batch, t2n, lanes), g2c, g2s)
    return out.reshape(batch * seq, d)


def _finish_kernel(x_ref, z_ref, wo_ref, m2_ref, m3_ref, m4_ref, m5_ref, g_ref, w1_ref, w2_ref, o_ref, *, hc):
    y = _dot(z_ref[...], wo_ref[...])
    t = x_ref[...] + m2_ref[0] * _rms(y, g_ref[1:2, :])
    h2 = (_rms(t, g_ref[2:3, :]) * (1.0 + m4_ref[0]) + m3_ref[0]).astype(BF16)
    acc = jnp.zeros(t.shape, F32)
    for n0 in range(0, w1_ref.shape[1], hc):
        a = jnp.maximum(_dot(h2, w1_ref[:, n0:n0 + hc]), 0.0)
        acc = acc + _dot((a * a).astype(BF16), w2_ref[n0:n0 + hc, :])
    o_ref[...] = t + m5_ref[0] * _rms(acc, g_ref[3:4, :])


def _finish(x2, z, w_out, mods, g, w1, w2, group_rows, tm):
    n_rows, d = x2.shape
    x_spec, m_spec = _row_specs(tm, d, group_rows // tm)
    kern = functools.partial(_finish_kernel, hc=1024)
    return pl.pallas_call(
        kern,
        grid=(n_rows // tm,),
        in_specs=[x_spec, pl.BlockSpec((tm, z.shape[1]), lambda i: (i, 0)), _full_spec(w_out.shape),
                  m_spec, m_spec, m_spec, m_spec, _full_spec(g.shape), _full_spec(w1.shape), _full_spec(w2.shape)],
        out_specs=x_spec,
        out_shape=jax.ShapeDtypeStruct((n_rows, d), F32),
        compiler_params=_cparams("parallel"),
        name="finish",
    )(x2, z, w_out, mods[2], mods[3], mods[4], mods[5], g, w1, w2)


def _split_mods(m_rows, d):
    return [m_rows[:, k * d:(k + 1) * d][:, None, :] for k in range(N_MOD)]


def kernel(x, c, ctx, c_ctx, mod_w, mod_b, norm_g, mlp_w1, mlp_w2, ssd_w_in, ssd_conv_w, ssd_conv_b, ssd_dt_bias,
           ssd_a_log, ssd_d, ssd_norm_g, ssd_w_out, na_w_qkv, na_rpb, na_w_out, sc_w_in, sc_conv_w, sc_w_out,
           fn_w_out):
    batch, seq, d = x.shape
    ctx_len = ctx.shape[1]
    depth = mod_w.shape[0]
    n_lat = batch * seq
    n_ctx = batch * ctx_len
    tm = min(256, ctx_len)

    c_rows = jnp.concatenate([c, c_ctx[None, :], jnp.zeros((8 - batch - 1, d), F32)], axis=0)
    mods_all = _modulation(c_rows, mod_w, mod_b)

    xl = x.reshape(n_lat, d)
    xc = ctx.reshape(n_ctx, d)

    for i in range(depth):
        kind, j = i % 4, i // 4
        m_lat = _split_mods(mods_all[i, :batch], d)
        m_ctx = _split_mods(mods_all[i, batch:batch + 1], d)
        g = norm_g[i]
        g0 = g[0:1]
        w1 = mlp_w1[i].astype(BF16)
        w2 = mlp_w2[i].astype(BF16)
        ctx_later = any((l % 4) in (0, 1) for l in range(i + 1, depth))
        zc = None
        if kind == 0:
            inner = ssd_w_out.shape[1]
            nh = ssd_a_log.shape[2]
            w_in = ssd_w_in[j].astype(BF16)
            n_conv = ssd_conv_w.shape[2]
            w_dt = w_in[:, inner + n_conv:]
            prm = (w_in[:, :inner], w_in[:, inner:inner + n_conv], w_dt, w_dt.T, ssd_conv_w[j],
                   ssd_conv_b[j][None, :], ssd_dt_bias[j].reshape(1, 2 * nh), ssd_dt_bias[j].reshape(2 * nh, 1))
            a_row = ssd_a_log[j].reshape(1, 2 * nh)
            a_col = ssd_a_log[j].reshape(2 * nh, 1)
            d_vec = jnp.repeat(ssd_d[j].sum(0), SSD_HEADDIM)[None, :]
            ng = ssd_norm_g[j][None, :]
            w_out = ssd_w_out[j].astype(BF16)
            tb = min(256, ctx_len)

            def mixer(rows2, seq_rows, group_rows, shift, scale, state0):
                z, xs, bcf, bcb, dt, dtt = _ssd_inproj(rows2, seq_rows, g0, shift, scale, group_rows, prm, tm)
                yf, yb, s_fin = _ssd_scan(xs, bcf, bcb, dt, dtt, a_row, a_col, state0, batch, tb)
                return _ssd_gate(yf, yb, xs, z, d_vec, ng, tm), s_fin

            zero_state = jnp.zeros((batch, 2, SSD_STATE, inner), F32)
            zc, ctx_state = mixer(xc, ctx_len, n_ctx, m_ctx[0], m_ctx[1], zero_state)
            zl, _ = mixer(xl, seq, seq, m_lat[0], m_lat[1], ctx_state)
        elif kind == 1:
            w_qkv = na_w_qkv[j].astype(BF16)
            w_out = na_w_out[j].astype(BF16)
            qc, kc, vc = _qkv(xc, g0, m_ctx[0], m_ctx[1], n_ctx, w_qkv, tm)
            q, k, v = _qkv(xl, g0, m_lat[0], m_lat[1], seq, w_qkv, tm)
            kc = kc.reshape(batch, ctx_len, d)
            vc = vc.reshape(batch, ctx_len, d)
            zl = _na_attention(q, k, v, kc, vc, _na_bias_table(na_rpb[j]), batch)
            if ctx_later:
                raise NotImplementedError("context self-attention is only needed when a later layer reads ctx")
        elif kind == 2:
            w_out = sc_w_out[j].astype(BF16)
            zl = _sconv(xl, seq, g0, m_lat[0], m_lat[1], sc_w_in[j].astype(BF16), sc_conv_w[j], tm)
            if ctx_later:
                zc = _sconv(xc, ctx_len, g0, m_ctx[0], m_ctx[1], sc_w_in[j].astype(BF16), sc_conv_w[j], tm)
        else:
            w_out = fn_w_out[j].astype(BF16)
            zl = _fourier(xl.reshape(batch, seq, d), g0, m_lat[0], m_lat[1])
            if ctx_later:
                raise NotImplementedError("context Fourier mixing is only needed when a later layer reads ctx")
        xl = _finish(xl, zl, w_out, m_lat, g, w1, w2, seq, tm)
        if ctx_later:
            xc = _finish(xc, zc, w_out, m_ctx, g, w1, w2, n_ctx, tm)
    return xl.reshape(batch, seq, d)
```

```python
import functools
import math

import jax
import jax.numpy as jnp
import numpy as np
from jax import lax
from jax.experimental import pallas as pl
from jax.experimental.pallas import tpu as pltpu

F32 = jnp.float32
BF16 = jnp.bfloat16
HIGHEST = lax.Precision.HIGHEST

RMS_EPS = 1e-6
N_MOD = 6
GRID_W = 64
HALO = 8
SSD_CHUNK = 128
SSD_HEADDIM = 64
SSD_STATE = 128
SSD_GROUPS = 4
NA_HEADS = 16
NA_HEADDIM = 64
NA_WIN_ROWS = 8
NA_WIN_COLS = 16
FN_GROUPS = 8
DFT_ROWS = 64
FN_SUB = 8
NEG_BIG = -1e30
VMEM_LIMIT = 56 * 1024 * 1024


def _cparams(*sem):
    return pltpu.CompilerParams(dimension_semantics=sem, vmem_limit_bytes=VMEM_LIMIT)


def _full_spec(shape):
    nd = len(shape)
    return pl.BlockSpec(shape, lambda *_: (0,) * nd)


def _rms(t, g):
    return t * lax.rsqrt(jnp.mean(t * t, axis=-1, keepdims=True) + RMS_EPS) * g


def _silu(t):
    return t * jax.nn.sigmoid(t)


def _softplus(t):
    return jnp.maximum(t, 0.0) + jnp.log1p(jnp.exp(-jnp.abs(t)))


def _dot(a, b, **kw):
    return jnp.dot(a, b, preferred_element_type=F32, **kw)


def _dot_nt(a, b, **kw):
    return lax.dot_general(a, b, (((1,), (1,)), ((), ())), preferred_element_type=F32, **kw)


def _mod_kernel(c_ref, w_ref, b_ref, o_ref):
    act = _silu(c_ref[...])
    o_ref[0] = _dot(act, w_ref[0], precision=HIGHEST) + b_ref[0]


def _modulation(c_rows, mod_w, mod_b):
    depth, d, n = mod_w.shape
    tn = n // 4
    rows = c_rows.shape[0]
    return pl.pallas_call(
        _mod_kernel,
        grid=(depth, n // tn),
        in_specs=[
            pl.BlockSpec((rows, d), lambda i, j: (0, 0)),
            pl.BlockSpec((1, d, tn), lambda i, j: (i, 0, j)),
            pl.BlockSpec((1, 1, tn), lambda i, j: (i, 0, j)),
        ],
        out_specs=pl.BlockSpec((1, rows, tn), lambda i, j: (i, 0, j)),
        out_shape=jax.ShapeDtypeStruct((depth, rows, n), F32),
        compiler_params=_cparams("parallel", "parallel"),
        name="modulation",
    )(c_rows, mod_w, mod_b.reshape(depth, 1, n))


def _row_specs(tm, d, tiles_per_group):
    x_spec = pl.BlockSpec((tm, d), lambda i: (i, 0))
    m_spec = pl.BlockSpec((1, 1, d), lambda i: (i // tiles_per_group, 0, 0))
    return x_spec, m_spec


def _halo_specs(tm, d, n_rows):
    per = tm // HALO
    last = n_rows // HALO - 1
    prev = pl.BlockSpec((HALO, d), lambda i: (jnp.maximum(i * per - 1, 0), 0))
    nxt = pl.BlockSpec((HALO, d), lambda i: (jnp.minimum((i + 1) * per, last), 0))
    return prev, nxt


def _normed_with_halo(xp_ref, x_ref, xn_ref, g_ref, sh_ref, sc_ref, tiles_per_seq):
    i = pl.program_id(0) % tiles_per_seq
    norm = lambda ref: _rms(ref[...], g_ref[...]) * (1.0 + sc_ref[0]) + sh_ref[0]
    keep_prev = jnp.where(i == 0, 0.0, 1.0)
    keep_next = jnp.where(i == tiles_per_seq - 1, 0.0, 1.0)
    return jnp.concatenate([norm(xp_ref) * keep_prev, norm(x_ref), norm(xn_ref) * keep_next], axis=0)


def _conv3(v, w_ref, tm):
    n = v.shape[0]
    prev = pltpu.roll(v, 1, 0)[HALO:HALO + tm]
    cur = v[HALO:HALO + tm]
    nxt = pltpu.roll(v, n - 1, 0)[HALO:HALO + tm]
    return prev * w_ref[0:1, :] + cur * w_ref[1:2, :] + nxt * w_ref[2:3, :]


def _ssd_inproj_kernel(xp_ref, x_ref, xn_ref, g_ref, sh_ref, sc_ref, wz_ref, wx_ref, wdt_ref, wdtt_ref,
                       cw_ref, cb_ref, dtb_ref, dtbt_ref,
                       z_ref, xs_ref, bcf_ref, bcb_ref, dt_ref, dtt_ref, *, tm, tiles_per_seq, inner):
    h_all = _normed_with_halo(xp_ref, x_ref, xn_ref, g_ref, sh_ref, sc_ref, tiles_per_seq)
    hb_all = h_all.astype(BF16)
    hb = h_all[HALO:HALO + tm].astype(BF16)

    z_ref[...] = _dot(hb, wz_ref[...])
    dt_ref[...] = _softplus(_dot(hb, wdt_ref[...]) + dtb_ref[...])
    dtt_ref[...] = _softplus(_dot_nt(wdtt_ref[...], hb) + dtbt_ref[...])

    cw = 1024
    n_ch = wx_ref.shape[1]
    for n0 in range(0, n_ch, cw):
        raw = _dot(hb_all, wx_ref[:, n0:n0 + cw])
        act = _silu(_conv3(raw, cw_ref.at[:, n0:n0 + cw], tm) + cb_ref[:, n0:n0 + cw])
        if n0 < inner:
            xs_ref[:, n0:n0 + cw] = act
        elif n0 < inner + cw:
            bcf_ref[...] = act.astype(BF16)
        else:
            bcb_ref[...] = act.astype(BF16)


def _ssd_inproj(x2, n_seq_rows, g0, shift, scale, tiles_per_group_rows, prm, tm):
    n_rows, d = x2.shape
    wz, wx, wdt, wdtt, conv_w, conv_b, dtb, dtbt = prm
    inner = wz.shape[1]
    nbc = (wx.shape[1] - inner) // 2
    nh2 = wdt.shape[1]
    x_spec, m_spec = _row_specs(tm, d, tiles_per_group_rows // tm)
    prev, nxt = _halo_specs(tm, d, n_rows)
    kern = functools.partial(_ssd_inproj_kernel, tm=tm, tiles_per_seq=n_seq_rows // tm, inner=inner)
    return pl.pallas_call(
        kern,
        grid=(n_rows // tm,),
        in_specs=[prev, x_spec, nxt, _full_spec((1, d)), m_spec, m_spec,
                  _full_spec(wz.shape), _full_spec(wx.shape), _full_spec(wdt.shape), _full_spec(wdtt.shape),
                  _full_spec(conv_w.shape), _full_spec(conv_b.shape), _full_spec(dtb.shape), _full_spec(dtbt.shape)],
        out_specs=[pl.BlockSpec((tm, inner), lambda i: (i, 0)),
                   pl.BlockSpec((tm, inner), lambda i: (i, 0)),
                   pl.BlockSpec((tm, nbc), lambda i: (i, 0)),
                   pl.BlockSpec((tm, nbc), lambda i: (i, 0)),
                   pl.BlockSpec((tm, nh2), lambda i: (i, 0)),
                   pl.BlockSpec((nh2, tm), lambda i: (0, i))],
        out_shape=[jax.ShapeDtypeStruct((n_rows, inner), F32),
                   jax.ShapeDtypeStruct((n_rows, inner), F32),
                   jax.ShapeDtypeStruct((n_rows, nbc), BF16),
                   jax.ShapeDtypeStruct((n_rows, nbc), BF16),
                   jax.ShapeDtypeStruct((n_rows, nh2), F32),
                   jax.ShapeDtypeStruct((nh2, n_rows), F32)],
        compiler_params=_cparams("parallel"),
        name="ssd_inproj",
    )(x2, x2, x2, g0, shift, scale, wz, wx, wdt, wdtt, conv_w, conv_b, dtb, dtbt)


def _ssd_chunk(d, r0, x_ref, bc_ref, dt_ref, dtt_ref, a_row, a_col, s_ref, y_ref, tri, mask, lane_lo):
    nh = a_row.shape[1] // 2
    hpg = nh // SSD_GROUPS
    rows = pl.ds(r0, SSD_CHUNK)
    dt = dt_ref[rows, :]
    dtt = dtt_ref[:, rows]
    cum = _dot(tri, dt * a_row, precision=HIGHEST)
    cumt = _dot_nt(dtt * a_col, tri, precision=HIGHEST)
    end = SSD_CHUNK - 1 if d == 0 else 0
    tot = cum[end:end + 1, :]
    tott = cumt[:, end:end + 1]
    wt = jnp.exp(tott - cumt) * dtt
    dec_s = jnp.exp(tot)
    for g in range(SSD_GROUPS):
        bg = bc_ref[rows, g * SSD_STATE:(g + 1) * SSD_STATE]
        cg = bc_ref[rows, (SSD_GROUPS + g) * SSD_STATE:(SSD_GROUPS + g + 1) * SSD_STATE]
        cb = _dot_nt(cg, bg)
        bt = bg.astype(F32).T
        cgf = cg.astype(F32)
        for p in range(hpg // 2):
            l0 = (g * hpg + 2 * p) * SSD_HEADDIM
            x2 = x_ref[rows, l0:l0 + 2 * SSD_HEADDIM]
            x2b = x2.astype(BF16)
            s2 = s_ref[0, d, :, l0:l0 + 2 * SSD_HEADDIM]
            rhs = jnp.concatenate([x2b, s2.astype(BF16)], axis=0)
            lhs_out, lhs_up, decs = [], [], []
            for hx in range(2):
                col = d * nh + g * hpg + 2 * p + hx
                colb = cum[:, col:col + 1]
                rowb = cumt[col:col + 1, :]
                dec = jnp.exp(jnp.where(mask, colb - rowb, NEG_BIG))
                sc = (cb * dec * dtt[col:col + 1, :]).astype(BF16)
                co = (cgf * jnp.exp(colb)).astype(BF16)
                lhs_out.append(jnp.concatenate([sc, co], axis=1))
                lhs_up.append((bt * wt[col:col + 1, :]).astype(BF16))
                decs.append(dec_s[:, col:col + 1])
            out2 = _dot(jnp.concatenate(lhs_out, axis=0), rhs)
            up2 = _dot(jnp.concatenate(lhs_up, axis=0), x2b)
            y_ref[rows, l0:l0 + 2 * SSD_HEADDIM] = jnp.where(lane_lo, out2[:SSD_CHUNK], out2[SSD_CHUNK:])
            s_ref[0, d, :, l0:l0 + 2 * SSD_HEADDIM] = (
                s2 * jnp.where(lane_lo, decs[0], decs[1])
                + jnp.where(lane_lo, up2[:SSD_STATE], up2[SSD_STATE:]))


def _ssd_scan_kernel(xf_ref, xb_ref, bcf_ref, bcb_ref, dtf_ref, dtb_ref, dttf_ref, dttb_ref,
                     al_ref, alt_ref, s0_ref, yf_ref, yb_ref, s_ref, *, n_chunk):
    @pl.when(pl.program_id(1) == 0)
    def _():
        s_ref[...] = s0_ref[...]

    a_row = -jnp.exp(al_ref[...])
    a_col = -jnp.exp(alt_ref[...])
    ii = lax.broadcasted_iota(jnp.int32, (SSD_CHUNK, SSD_CHUNK), 0)
    jj = lax.broadcasted_iota(jnp.int32, (SSD_CHUNK, SSD_CHUNK), 1)
    lane_lo = lax.broadcasted_iota(jnp.int32, (1, 2 * SSD_HEADDIM), 1) < SSD_HEADDIM
    for c in range(n_chunk):
        low = jj <= ii
        _ssd_chunk(0, c * SSD_CHUNK, xf_ref, bcf_ref, dtf_ref, dttf_ref, a_row, a_col, s_ref, yf_ref,
                   jnp.where(low, 1.0, 0.0), low, lane_lo)
        up = jj >= ii
        _ssd_chunk(1, (n_chunk - 1 - c) * SSD_CHUNK, xb_ref, bcb_ref, dtb_ref, dttb_ref, a_row, a_col, s_ref,
                   yb_ref, jnp.where(up, 1.0, 0.0), up, lane_lo)


def _ssd_scan(xs, bcf, bcb, dt, dtt, a_log_row, a_log_col, state0, batch, tb):
    n_rows, inner = xs.shape
    nbc = bcf.shape[1]
    nh2 = dt.shape[1]
    nb = n_rows // batch // tb
    fwd = lambda b, k: (b * nb + k, 0)
    bwd = lambda b, k: (b * nb + nb - 1 - k, 0)
    fwd_t = lambda b, k: (0, b * nb + k)
    bwd_t = lambda b, k: (0, b * nb + nb - 1 - k)
    s_spec = pl.BlockSpec((1, 2, SSD_STATE, inner), lambda b, k: (b, 0, 0, 0))
    kern = functools.partial(_ssd_scan_kernel, n_chunk=tb // SSD_CHUNK)
    return pl.pallas_call(
        kern,
        grid=(batch, nb),
        in_specs=[pl.BlockSpec((tb, inner), fwd), pl.BlockSpec((tb, inner), bwd),
                  pl.BlockSpec((tb, nbc), fwd), pl.BlockSpec((tb, nbc), bwd),
                  pl.BlockSpec((tb, nh2), fwd), pl.BlockSpec((tb, nh2), bwd),
                  pl.BlockSpec((nh2, tb), fwd_t), pl.BlockSpec((nh2, tb), bwd_t),
                  _full_spec(a_log_row.shape), _full_spec(a_log_col.shape), s_spec],
        out_specs=[pl.BlockSpec((tb, inner), fwd), pl.BlockSpec((tb, inner), bwd), s_spec],
        out_shape=[jax.ShapeDtypeStruct((n_rows, inner), F32),
                   jax.ShapeDtypeStruct((n_rows, inner), F32),
                   jax.ShapeDtypeStruct(state0.shape, F32)],
        compiler_params=_cparams("parallel", "arbitrary"),
        name="ssd_scan",
    )(xs, xs, bcf, bcb, dt, dt, dtt, dtt, a_log_row, a_log_col, state0)


def _ssd_gate_kernel(yf_ref, yb_ref, xs_ref, z_ref, d_ref, g_ref, o_ref):
    y = yf_ref[...] + yb_ref[...] + d_ref[...] * xs_ref[...]
    y = y * _silu(z_ref[...])
    o_ref[...] = _rms(y, g_ref[...]).astype(BF16)


def _ssd_gate(yf, yb, xs, z, d_vec, norm_g, tm):
    n_rows, inner = yf.shape
    spec = pl.BlockSpec((tm, inner), lambda i: (i, 0))
    return pl.pallas_call(
        _ssd_gate_kernel,
        grid=(n_rows // tm,),
        in_specs=[spec, spec, spec, spec, _full_spec((1, inner)), _full_spec((1, inner))],
        out_specs=spec,
        out_shape=jax.ShapeDtypeStruct((n_rows, inner), BF16),
        compiler_params=_cparams("parallel"),
        name="ssd_gate",
    )(yf, yb, xs, z, d_vec, norm_g)


def _qkv_kernel(x_ref, g_ref, sh_ref, sc_ref, w_ref, q_ref, k_ref, v_ref):
    d = x_ref.shape[1]
    hb = (_rms(x_ref[...], g_ref[...]) * (1.0 + sc_ref[0]) + sh_ref[0]).astype(BF16)
    q_ref[...] = (_dot(hb, w_ref[:, 0:d]) * (NA_HEADDIM ** -0.5)).astype(BF16)
    k_ref[...] = _dot(hb, w_ref[:, d:2 * d]).astype(BF16)
    v_ref[...] = _dot(hb, w_ref[:, 2 * d:3 * d]).astype(BF16)


def _qkv(x2, g0, shift, scale, group_rows, w, tm):
    n_rows, d = x2.shape
    x_spec, m_spec = _row_specs(tm, d, group_rows // tm)
    out = jax.ShapeDtypeStruct((n_rows, d), BF16)
    return pl.pallas_call(
        _qkv_kernel,
        grid=(n_rows // tm,),
        in_specs=[x_spec, _full_spec((1, d)), m_spec, m_spec, _full_spec(w.shape)],
        out_specs=[x_spec, x_spec, x_spec],
        out_shape=[out, out, out],
        compiler_params=_cparams("parallel"),
        name="na_qkv",
    )(x2, g0, shift, scale, w)


def _na_kernel(*refs):
    q_ref = refs[0]
    k_refs = refs[1:1 + NA_WIN_ROWS]
    v_refs = refs[1 + NA_WIN_ROWS:1 + 2 * NA_WIN_ROWS]
    kc_ref, vc_ref, bias_ref, o_ref = refs[1 + 2 * NA_WIN_ROWS:]
    r = pl.program_id(1)
    rows = pl.num_programs(1)
    ro0 = jnp.clip(r - NA_WIN_ROWS // 2, 0, rows - NA_WIN_ROWS) - r + NA_WIN_ROWS - 1
    lane_lo = lax.broadcasted_iota(jnp.int32, (1, 2 * NA_HEADDIM), 1) < NA_HEADDIM
    pair_lanes = [slice(p * 2 * NA_HEADDIM, (p + 1) * 2 * NA_HEADDIM) for p in range(NA_HEADS // 2)]
    scores = []
    for p, lanes in enumerate(pair_lanes):
        q2 = q_ref[:, lanes]
        k2 = jnp.concatenate([kr[:, lanes] for kr in k_refs], axis=0)
        kc2 = kc_ref[0, :, lanes]
        for hx in range(2):
            keep = lane_lo if hx == 0 else jnp.logical_not(lane_lo)
            qh = jnp.where(keep, q2, jnp.zeros_like(q2))
            scores.append((_dot_nt(qh, k2), _dot_nt(qh, kc2)))
    probs = []
    for h, (s_loc, s_ctx) in enumerate(scores):
        bias = jnp.concatenate([bias_ref[h, ro0 + 2 * w2] for w2 in range(NA_WIN_ROWS // 2)], axis=1)
        s_loc = s_loc + bias
        m = jnp.maximum(jnp.max(s_loc, axis=-1, keepdims=True), jnp.max(s_ctx, axis=-1, keepdims=True))
        p_loc = jnp.exp(s_loc - m)
        p_ctx = jnp.exp(s_ctx - m)
        den = jnp.sum(p_loc, axis=-1, keepdims=True) + jnp.sum(p_ctx, axis=-1, keepdims=True)
        probs.append((p_loc.astype(BF16), p_ctx.astype(BF16), den))
    for p, lanes in enumerate(pair_lanes):
        v2 = jnp.concatenate([vr[:, lanes] for vr in v_refs], axis=0)
        vc2 = vc_ref[0, :, lanes]
        outs = []
        for hx in range(2):
            p_loc, p_ctx, den = probs[2 * p + hx]
            outs.append((_dot(p_loc, v2) + _dot(p_ctx, vc2)) / den)
        o_ref[:, lanes] = jnp.where(lane_lo, outs[0], outs[1]).astype(BF16)


def _na_bias_table(rpb):
    n_heads, n_ro, n_co = rpb.shape
    side = GRID_W - NA_WIN_COLS
    vp = jnp.pad(rpb, ((0, 0), (0, 0), (side, side + 1)), constant_values=NEG_BIG)
    width = n_co + 2 * side
    skew = jnp.broadcast_to(vp[:, :, None, :], (n_heads, n_ro, GRID_W, width + 1))
    skew = skew.reshape(n_heads, n_ro, GRID_W * (width + 1))[:, :, :GRID_W * width]
    toe = skew.reshape(n_heads, n_ro, GRID_W, width)[:, :, :, GRID_W - 1:]
    cols = np.arange(GRID_W)
    c0 = np.clip(cols - NA_WIN_COLS // 2, 0, GRID_W - NA_WIN_COLS)
    valid = (cols[None, :] >= c0[:, None]) & (cols[None, :] < c0[:, None] + NA_WIN_COLS)
    toe = jnp.where(valid[None, None], toe, NEG_BIG)
    return jnp.concatenate([toe[:, :-1], toe[:, 1:]], axis=-1)


def _na_attention(q, k, v, kc, vc, bias_tab, batch):
    n_rows, d = q.shape
    rows = n_rows // batch // GRID_W
    wr = NA_WIN_ROWS
    start = lambda r: jnp.clip(r - wr // 2, 0, rows - wr)
    q_spec = pl.BlockSpec((GRID_W, d), lambda b, r: (b * rows + r, 0))
    win = [pl.BlockSpec((GRID_W, d), functools.partial(lambda b, r, w: (b * rows + start(r) + w, 0), w=w))
           for w in range(wr)]
    c_spec = pl.BlockSpec((1,) + kc.shape[1:], lambda b, r: (b, 0, 0))
    return pl.pallas_call(
        _na_kernel,
        grid=(batch, rows),
        in_specs=[q_spec] + win + win + [c_spec, c_spec, _full_spec(bias_tab.shape)],
        out_specs=q_spec,
        out_shape=jax.ShapeDtypeStruct((n_rows, d), BF16),
        compiler_params=_cparams("parallel", "arbitrary"),
        name="na_attention",
    )(q, *([k] * wr), *([v] * wr), kc, vc, bias_tab)


def _sconv_kernel(xp_ref, x_ref, xn_ref, g_ref, sh_ref, sc_ref, w_ref, cw_ref, o_ref, *, tm, tiles_per_seq):
    d = x_ref.shape[1]
    h_all = _normed_with_halo(xp_ref, x_ref, xn_ref, g_ref, sh_ref, sc_ref, tiles_per_seq)
    hb_all = h_all.astype(BF16)
    hb = h_all[HALO:HALO + tm].astype(BF16)
    cw = 512
    for n0 in range(0, d, cw):
        c_gate = _dot(hb_all, w_ref[:, d + n0:d + n0 + cw])
        u = _dot(hb_all, w_ref[:, 2 * d + n0:2 * d + n0 + cw])
        conv = _conv3(c_gate * u, cw_ref.at[:, n0:n0 + cw], tm)
        b_gate = _dot(hb, w_ref[:, n0:n0 + cw])
        o_ref[:, n0:n0 + cw] = (b_gate * conv).astype(BF16)


def _sconv(x2, seq_rows, g0, shift, scale, w, conv_w, tm):
    n_rows, d = x2.shape
    x_spec, m_spec = _row_specs(tm, d, seq_rows // tm)
    prev, nxt = _halo_specs(tm, d, n_rows)
    kern = functools.partial(_sconv_kernel, tm=tm, tiles_per_seq=seq_rows // tm)
    return pl.pallas_call(
        kern,
        grid=(n_rows // tm,),
        in_specs=[prev, x_spec, nxt, _full_spec((1, d)), m_spec, m_spec, _full_spec(w.shape),
                  _full_spec(conv_w.shape)],
        out_specs=x_spec,
        out_shape=jax.ShapeDtypeStruct((n_rows, d), BF16),
        compiler_params=_cparams("parallel"),
        name="sconv",
    )(x2, x2, x2, g0, shift, scale, w, conv_w)


def _fourier1_kernel(x_ref, g_ref, sh_ref, sc_ref, cs_ref, km_ref, twc_ref, tws_ref, yr_ref, yi_ref):
    d = g_ref.shape[1]
    gc = d // FN_GROUPS
    n = DFT_ROWS * FN_SUB
    x = x_ref[0].reshape(n, d)
    hb = (_rms(x, g_ref[...]) * (1.0 + sc_ref[0]) + sh_ref[0]).astype(BF16)
    zr, zi = [], []
    for g in range(FN_GROUPS):
        zz = _dot(hb[:, g * gc:(g + 1) * gc], cs_ref[...])
        zr.append(zz[:, :gc])
        zi.append(zz[:, gc:])
    z = jnp.concatenate([jnp.concatenate(zr, axis=1), jnp.concatenate(zi, axis=1)], axis=0)
    y = _dot(km_ref[...], z.astype(BF16))
    yr, yi = y[:n], y[n:]
    tc, ts = twc_ref[0], tws_ref[0]
    yr_ref[0] = (yr * tc + yi * ts).reshape(FN_SUB, DFT_ROWS, d)
    yi_ref[0] = (yi * tc - yr * ts).reshape(FN_SUB, DFT_ROWS, d)


def _fourier2_kernel(yr_ref, yi_ref, gk_ref, o_ref):
    t2n, _, d = yr_ref.shape[1:]
    n = t2n * FN_SUB
    y = jnp.concatenate([yr_ref[0].reshape(n, d), yi_ref[0].reshape(n, d)], axis=0)
    o_ref[0] = _dot(gk_ref[...], y.astype(BF16)).reshape(t2n, FN_SUB, d)


def _dft_tables(seq, gc):
    t2n = seq // DFT_ROWS
    eye = np.eye(FN_SUB)
    c = np.arange(gc)
    ang_c = 2.0 * np.pi * np.outer(c, c) / gc
    cs = np.concatenate([np.cos(ang_c), -np.sin(ang_c)], axis=1)
    k1 = np.arange(DFT_ROWS)
    ang1 = 2.0 * np.pi * np.outer(k1, k1) / DFT_ROWS
    c1, s1 = np.cos(ang1), np.sin(ang1)
    m = np.stack([np.stack([c1, s1], axis=1), np.stack([-s1, c1], axis=1)], axis=0)
    km = np.einsum("akbt,sr->askbtr", m, eye).reshape(2 * DFT_ROWS * FN_SUB, 2 * DFT_ROWS * FN_SUB)
    t2 = np.arange(t2n)
    ang_tw = 2.0 * np.pi * np.outer(t2, k1) / seq
    tw_shape = (t2n // FN_SUB, FN_SUB * DFT_ROWS, 1)
    ang2 = 2.0 * np.pi * np.outer(t2, t2) / t2n
    g2 = np.stack([np.cos(ang2), np.sin(ang2)], axis=1)
    gk = np.einsum("kbt,sr->ksbtr", g2, eye).reshape(t2n * FN_SUB, 2 * t2n * FN_SUB)
    return (jnp.asarray(cs, BF16), jnp.asarray(km, BF16),
            jnp.asarray(np.cos(ang_tw).reshape(tw_shape), F32), jnp.asarray(np.sin(ang_tw).reshape(tw_shape), F32),
            jnp.asarray(gk, BF16))


def _fourier(x3, g0, shift, scale):
    batch, seq, d = x3.shape
    t2n = seq // DFT_ROWS
    n1 = DFT_ROWS * FN_SUB
    cs, km, twc, tws, gk = _dft_tables(seq, d // FN_GROUPS)
    xv = x3.reshape(batch, DFT_ROWS, t2n, d)
    m_spec = pl.BlockSpec((1, 1, d), lambda b, j: (b, 0, 0))
    y1_spec = pl.BlockSpec((1, FN_SUB, DFT_ROWS, d), lambda b, j: (b, j, 0, 0))
    tw_spec = pl.BlockSpec((1, n1, 1), lambda b, j: (j, 0, 0))
    y_shape = jax.ShapeDtypeStruct((batch, t2n, DFT_ROWS, d), F32)
    yr, yi = pl.pallas_call(
        _fourier1_kernel,
        grid=(batch, t2n // FN_SUB),
        in_specs=[pl.BlockSpec((1, DFT_ROWS, FN_SUB, d), lambda b, j: (b, 0, j, 0)),
                  _full_spec((1, d)), m_spec, m_spec, _full_spec(cs.shape), _full_spec(km.shape), tw_spec, tw_spec],
        out_specs=[y1_spec, y1_spec],
        out_shape=[y_shape, y_shape],
        compiler_params=_cparams("parallel", "parallel"),
        name="fourier_stage1",
    )(xv, g0, shift, scale, cs, km, twc, tws)
    y2_spec = pl.BlockSpec((1, t2n, FN_SUB, d), lambda b, j: (b, 0, j, 0))
    out = pl.pallas_call(
        _fourier2_kernel,
        grid=(batch, DFT_ROWS // FN_SUB),
        in_specs=[y2_spec, y2_spec, _full_spec(gk.shape)],
        out_specs=y2_spec,
        out_shape=y_shape,
        compiler_params=_cparams("parallel", "parallel"),
        name="fourier_stage2",
    )(yr, yi, gk)
    return out.reshape(batch * seq, d)


def _finish_kernel(x_ref, z_ref, wo_ref, m2_ref, m3_ref, m4_ref, m5_ref, g_ref, w1_ref, w2_ref, o_ref, *, hc):
    y = _dot(z_ref[...].astype(BF16), wo_ref[...])
    t = x_ref[...] + m2_ref[0] * _rms(y, g_ref[1:2, :])
    h2 = (_rms(t, g_ref[2:3, :]) * (1.0 + m4_ref[0]) + m3_ref[0]).astype(BF16)
    starts = list(range(0, w1_ref.shape[1], hc))
    up = _dot(h2, w1_ref[:, 0:hc])
    acc = None
    for n0 in starts:
        nxt = _dot(h2, w1_ref[:, n0 + hc:n0 + 2 * hc]) if n0 != starts[-1] else None
        a = jnp.maximum(up, 0.0)
        down = _dot((a * a).astype(BF16), w2_ref[n0:n0 + hc, :])
        acc = down if acc is None else acc + down
        up = nxt
    o_ref[...] = t + m5_ref[0] * _rms(acc, g_ref[3:4, :])


def _finish(x2, z, w_out, mods, g, w1, w2, group_rows, tm):
    n_rows, d = x2.shape
    x_spec, m_spec = _row_specs(tm, d, group_rows // tm)
    kern = functools.partial(_finish_kernel, hc=1024)
    return pl.pallas_call(
        kern,
        grid=(n_rows // tm,),
        in_specs=[x_spec, pl.BlockSpec((tm, z.shape[1]), lambda i: (i, 0)), _full_spec(w_out.shape),
                  m_spec, m_spec, m_spec, m_spec, _full_spec(g.shape), _full_spec(w1.shape), _full_spec(w2.shape)],
        out_specs=x_spec,
        out_shape=jax.ShapeDtypeStruct((n_rows, d), F32),
        compiler_params=_cparams("parallel"),
        name="finish",
    )(x2, z, w_out, mods[2], mods[3], mods[4], mods[5], g, w1, w2)


def _split_mods(m_rows, d):
    return [m_rows[:, k * d:(k + 1) * d][:, None, :] for k in range(N_MOD)]


def kernel(x, c, ctx, c_ctx, mod_w, mod_b, norm_g, mlp_w1, mlp_w2, ssd_w_in, ssd_conv_w, ssd_conv_b, ssd_dt_bias,
           ssd_a_log, ssd_d, ssd_norm_g, ssd_w_out, na_w_qkv, na_rpb, na_w_out, sc_w_in, sc_conv_w, sc_w_out,
           fn_w_out):
    batch, seq, d = x.shape
    ctx_len = ctx.shape[1]
    depth = mod_w.shape[0]
    n_lat = batch * seq
    n_ctx = batch * ctx_len
    tm = min(256, ctx_len)

    c_rows = jnp.concatenate([c, c_ctx[None, :], jnp.zeros((8 - batch - 1, d), F32)], axis=0)
    mods_all = _modulation(c_rows, mod_w, mod_b)

    xl = x.reshape(n_lat, d)
    xc = ctx.reshape(n_ctx, d)

    for i in range(depth):
        kind, j = i % 4, i // 4
        m_lat = _split_mods(mods_all[i, :batch], d)
        m_ctx = _split_mods(mods_all[i, batch:batch + 1], d)
        g = norm_g[i]
        g0 = g[0:1]
        w1 = mlp_w1[i].astype(BF16)
        w2 = mlp_w2[i].astype(BF16)
        ctx_later = any((l % 4) in (0, 1) for l in range(i + 1, depth))
        zc = None
        if kind == 0:
            inner = ssd_w_out.shape[1]
            nh = ssd_a_log.shape[2]
            w_in = ssd_w_in[j].astype(BF16)
            n_conv = ssd_conv_w.shape[2]
            w_dt = w_in[:, inner + n_conv:]
            prm = (w_in[:, :inner], w_in[:, inner:inner + n_conv], w_dt, w_dt.T, ssd_conv_w[j],
                   ssd_conv_b[j][None, :], ssd_dt_bias[j].reshape(1, 2 * nh), ssd_dt_bias[j].reshape(2 * nh, 1))
            a_row = ssd_a_log[j].reshape(1, 2 * nh)
            a_col = ssd_a_log[j].reshape(2 * nh, 1)
            d_vec = jnp.repeat(ssd_d[j].sum(0), SSD_HEADDIM)[None, :]
            ng = ssd_norm_g[j][None, :]
            w_out = ssd_w_out[j].astype(BF16)
            tb = min(256, ctx_len)

            def mixer(rows2, seq_rows, group_rows, shift, scale, state0):
                z, xs, bcf, bcb, dt, dtt = _ssd_inproj(rows2, seq_rows, g0, shift, scale, group_rows, prm, tm)
                yf, yb, s_fin = _ssd_scan(xs, bcf, bcb, dt, dtt, a_row, a_col, state0, batch, tb)
                return _ssd_gate(yf, yb, xs, z, d_vec, ng, tm), s_fin

            zero_state = jnp.zeros((batch, 2, SSD_STATE, inner), F32)
            zc, ctx_state = mixer(xc, ctx_len, n_ctx, m_ctx[0], m_ctx[1], zero_state)
            zl, _ = mixer(xl, seq, seq, m_lat[0], m_lat[1], ctx_state)
        elif kind == 1:
            w_qkv = na_w_qkv[j].astype(BF16)
            w_out = na_w_out[j].astype(BF16)
            qc, kc, vc = _qkv(xc, g0, m_ctx[0], m_ctx[1], n_ctx, w_qkv, tm)
            q, k, v = _qkv(xl, g0, m_lat[0], m_lat[1], seq, w_qkv, tm)
            kc = kc.reshape(batch, ctx_len, d)
            vc = vc.reshape(batch, ctx_len, d)
            zl = _na_attention(q, k, v, kc, vc, _na_bias_table(na_rpb[j]), batch)
            if ctx_later:
                raise NotImplementedError("context self-attention is only needed when a later layer reads ctx")
        elif kind == 2:
            w_out = sc_w_out[j].astype(BF16)
            zl = _sconv(xl, seq, g0, m_lat[0], m_lat[1], sc_w_in[j].astype(BF16), sc_conv_w[j], tm)
            if ctx_later:
                zc = _sconv(xc, ctx_len, g0, m_ctx[0], m_ctx[1], sc_w_in[j].astype(BF16), sc_conv_w[j], tm)
        else:
            w_out = fn_w_out[j].astype(BF16)
            zl = _fourier(xl.reshape(batch, seq, d), g0, m_lat[0], m_lat[1])
            if ctx_later:
                raise NotImplementedError("context Fourier mixing is only needed when a later layer reads ctx")
        xl = _finish(xl, zl, w_out, m_lat, g, w1, w2, seq, tm)
        if ctx_later:
            xc = _finish(xc, zc, w_out, m_ctx, g, w1, w2, n_ctx, tm)
    return xl.reshape(batch, seq, d)
```

```python
import functools
import math

import jax
import jax.numpy as jnp
import numpy as np
from jax import lax
from jax.experimental import pallas as pl
from jax.experimental.pallas import tpu as pltpu

F32 = jnp.float32
BF16 = jnp.bfloat16
HIGHEST = lax.Precision.HIGHEST

RMS_EPS = 1e-6
N_MOD = 6
GRID_W = 64
HALO = 8
SSD_CHUNK = 128
SSD_HEADDIM = 64
SSD_STATE = 128
SSD_GROUPS = 4
NA_HEADS = 16
NA_HEADDIM = 64
NA_WIN_ROWS = 8
NA_WIN_COLS = 16
NA_STAGE_PAIRS = 4
FN_GROUPS = 8
DFT_ROWS = 64
FN_SUB = 8
NEG_BIG = -1e30
LOG2_E = math.log2(math.e)
VMEM_LIMIT = 56 * 1024 * 1024


def _cparams(*sem):
    return pltpu.CompilerParams(dimension_semantics=sem, vmem_limit_bytes=VMEM_LIMIT)


def _full_spec(shape):
    nd = len(shape)
    return pl.BlockSpec(shape, lambda *_: (0,) * nd)


def _rms(t, g):
    return t * lax.rsqrt(jnp.mean(t * t, axis=-1, keepdims=True) + RMS_EPS) * g


def _silu(t):
    half = 0.5 * t
    return half + half * jnp.tanh(half)


def _softplus(t):
    return jnp.maximum(t, 0.0) + jnp.log1p(jnp.exp(-jnp.abs(t)))


def _dot(a, b, **kw):
    return jnp.dot(a, b, preferred_element_type=F32, **kw)


def _dot_nt(a, b, **kw):
    return lax.dot_general(a, b, (((1,), (1,)), ((), ())), preferred_element_type=F32, **kw)


def _mod_kernel(c_ref, w_ref, b_ref, o_ref):
    act = _silu(c_ref[...]).astype(BF16)
    o_ref[0] = _dot(act, w_ref[0].astype(BF16)) + b_ref[0]


def _modulation(c_rows, mod_w, mod_b):
    depth, d, n = mod_w.shape
    tn = n // 4
    rows = c_rows.shape[0]
    return pl.pallas_call(
        _mod_kernel,
        grid=(depth, n // tn),
        in_specs=[
            pl.BlockSpec((rows, d), lambda i, j: (0, 0)),
            pl.BlockSpec((1, d, tn), lambda i, j: (i, 0, j)),
            pl.BlockSpec((1, 1, tn), lambda i, j: (i, 0, j)),
        ],
        out_specs=pl.BlockSpec((1, rows, tn), lambda i, j: (i, 0, j)),
        out_shape=jax.ShapeDtypeStruct((depth, rows, n), F32),
        compiler_params=_cparams("parallel", "parallel"),
        name="modulation",
    )(c_rows, mod_w, mod_b.reshape(depth, 1, n))


def _row_specs(tm, d, tiles_per_group):
    x_spec = pl.BlockSpec((tm, d), lambda i: (i, 0))
    m_spec = pl.BlockSpec((1, 1, d), lambda i: (i // tiles_per_group, 0, 0))
    return x_spec, m_spec


def _halo_specs(tm, d, n_rows):
    per = tm // HALO
    last = n_rows // HALO - 1
    prev = pl.BlockSpec((HALO, d), lambda i: (jnp.maximum(i * per - 1, 0), 0))
    nxt = pl.BlockSpec((HALO, d), lambda i: (jnp.minimum((i + 1) * per, last), 0))
    return prev, nxt


def _normed_with_halo(xp_ref, x_ref, xn_ref, g_ref, sh_ref, sc_ref, tiles_per_seq):
    i = pl.program_id(0) % tiles_per_seq
    norm = lambda ref: _rms(ref[...], g_ref[...]) * (1.0 + sc_ref[0]) + sh_ref[0]
    keep_prev = jnp.where(i == 0, 0.0, 1.0)
    keep_next = jnp.where(i == tiles_per_seq - 1, 0.0, 1.0)
    return jnp.concatenate([norm(xp_ref) * keep_prev, norm(x_ref), norm(xn_ref) * keep_next], axis=0)


def _conv3(v_ref, w_ref, tm):
    prev = v_ref[pl.ds(HALO - 1, tm), :]
    cur = v_ref[pl.ds(HALO, tm), :]
    nxt = v_ref[pl.ds(HALO + 1, tm), :]
    return prev * w_ref[0:1, :] + cur * w_ref[1:2, :] + nxt * w_ref[2:3, :]


SSD_CONV_TILE = 1024


def _ssd_inproj_kernel(xp_ref, x_ref, xn_ref, g_ref, sh_ref, sc_ref, w_ref, wdtt_ref, cw_ref, cb_ref, dtbt_ref,
                       z_ref, xs_ref, bcf_ref, bcb_ref, dt_ref, dtt_ref, raw_ref, *, tm, tiles_per_seq, inner):
    h_all = _normed_with_halo(xp_ref, x_ref, xn_ref, g_ref, sh_ref, sc_ref, tiles_per_seq)
    hb_all = h_all.astype(BF16)
    hb = h_all[HALO:HALO + tm].astype(BF16)

    z_ref[...] = _dot(hb, w_ref[:, 0:inner]).astype(BF16)
    nh2 = dtt_ref.shape[0]
    dtt = _softplus(_dot_nt(wdtt_ref[...], hb) + dtbt_ref[...])
    dtt_ref[...] = dtt[:nh2]
    dt_ref[...] = dtt.T[:, :nh2]

    cw = SSD_CONV_TILE
    for n0 in range(0, cw_ref.shape[1], cw):
        raw_ref[...] = _dot(hb_all, w_ref[:, inner + n0:inner + n0 + cw])
        act = _silu(_conv3(raw_ref, cw_ref.at[:, n0:n0 + cw], tm) + cb_ref[:, n0:n0 + cw]).astype(BF16)
        if n0 < inner:
            xs_ref[:, n0:n0 + cw] = act
        elif n0 < inner + cw:
            bcf_ref[...] = act
        else:
            bcb_ref[...] = act


def _ssd_inproj(x2, n_seq_rows, g0, shift, scale, tiles_per_group_rows, prm, tm):
    n_rows, d = x2.shape
    w_in, wdtt, conv_w, conv_b, dtbt, inner, nh2 = prm
    nbc = (conv_w.shape[1] - inner) // 2
    assert nbc == SSD_CONV_TILE and inner % SSD_CONV_TILE == 0
    x_spec, m_spec = _row_specs(tm, d, tiles_per_group_rows // tm)
    prev, nxt = _halo_specs(tm, d, n_rows)
    kern = functools.partial(_ssd_inproj_kernel, tm=tm, tiles_per_seq=n_seq_rows // tm, inner=inner)
    row_spec = lambda n: pl.BlockSpec((tm, n), lambda i: (i, 0))
    return pl.pallas_call(
        kern,
        grid=(n_rows // tm,),
        in_specs=[prev, x_spec, nxt, _full_spec((1, d)), m_spec, m_spec, _full_spec(w_in.shape),
                  _full_spec(wdtt.shape), _full_spec(conv_w.shape), _full_spec(conv_b.shape), _full_spec(dtbt.shape)],
        out_specs=[row_spec(inner), row_spec(inner), row_spec(nbc), row_spec(nbc), row_spec(nh2),
                   pl.BlockSpec((nh2, tm), lambda i: (0, i))],
        out_shape=[jax.ShapeDtypeStruct((n_rows, inner), BF16),
                   jax.ShapeDtypeStruct((n_rows, inner), BF16),
                   jax.ShapeDtypeStruct((n_rows, nbc), BF16),
                   jax.ShapeDtypeStruct((n_rows, nbc), BF16),
                   jax.ShapeDtypeStruct((n_rows, nh2), F32),
                   jax.ShapeDtypeStruct((nh2, n_rows), F32)],
        scratch_shapes=[pltpu.VMEM((tm + 2 * HALO, SSD_CONV_TILE), F32)],
        compiler_params=_cparams("parallel"),
        name="ssd_inproj",
    )(x2, x2, x2, g0, shift, scale, w_in, wdtt, conv_w, conv_b, dtbt)


def _ssd_chunk(d, r0, x_ref, bc_ref, dt_ref, dtt_ref, a_row, a_col, s_ref, y_ref, tri, mask, lane_lo):
    nh = a_row.shape[1] // 2
    hpg = nh // SSD_GROUPS
    rows = pl.ds(r0, SSD_CHUNK)
    dt = dt_ref[rows, :]
    dtt = dtt_ref[:, rows]
    cum = _dot(tri, dt * a_row, precision=HIGHEST) * LOG2_E
    cumt = _dot_nt(dtt * a_col, tri, precision=HIGHEST) * LOG2_E
    end = SSD_CHUNK - 1 if d == 0 else 0
    tot = cum[end:end + 1, :]
    tott = cumt[:, end:end + 1]
    wt = jnp.exp2(tott - cumt) * dtt
    dec_s = jnp.exp2(tot)
    cumt_dt = cumt - jnp.log2(dtt)
    for g in range(SSD_GROUPS):
        bg = bc_ref[rows, g * SSD_STATE:(g + 1) * SSD_STATE]
        cg = bc_ref[rows, (SSD_GROUPS + g) * SSD_STATE:(SSD_GROUPS + g + 1) * SSD_STATE]
        cb = _dot_nt(cg, bg)
        bt = bg.astype(F32).T
        cgf = cg.astype(F32)
        for p in range(hpg // 2):
            l0 = (g * hpg + 2 * p) * SSD_HEADDIM
            x2b = x_ref[rows, l0:l0 + 2 * SSD_HEADDIM]
            s2 = s_ref[0, d, :, l0:l0 + 2 * SSD_HEADDIM]
            rhs = jnp.concatenate([x2b, s2.astype(BF16)], axis=0)
            lhs_out, lhs_up, decs = [], [], []
            for hx in range(2):
                col = d * nh + g * hpg + 2 * p + hx
                colb = jnp.broadcast_to(cum[:, col:col + 1], (SSD_CHUNK, SSD_CHUNK))
                rowb = cumt_dt[col:col + 1, :]
                sc = (cb * jnp.exp2(jnp.where(mask, colb - rowb, NEG_BIG))).astype(BF16)
                co = (cgf * jnp.exp2(colb)).astype(BF16)
                lhs_out.append(jnp.concatenate([sc, co], axis=1))
                lhs_up.append((bt * wt[col:col + 1, :]).astype(BF16))
                decs.append(dec_s[:, col:col + 1])
            out2 = _dot(jnp.concatenate(lhs_out, axis=0), rhs)
            up2 = _dot(jnp.concatenate(lhs_up, axis=0), x2b)
            y_ref[rows, l0:l0 + 2 * SSD_HEADDIM] = jnp.where(
                lane_lo, out2[:SSD_CHUNK], out2[SSD_CHUNK:]).astype(BF16)
            s_ref[0, d, :, l0:l0 + 2 * SSD_HEADDIM] = (
                s2 * jnp.where(lane_lo, decs[0], decs[1])
                + jnp.where(lane_lo, up2[:SSD_STATE], up2[SSD_STATE:]))


def _ssd_scan_kernel(xf_ref, xb_ref, bcf_ref, bcb_ref, dtf_ref, dtb_ref, dttf_ref, dttb_ref,
                     al_ref, alt_ref, s0_ref, yf_ref, yb_ref, s_ref, *, n_chunk):
    @pl.when(pl.program_id(1) == 0)
    def _():
        s_ref[...] = s0_ref[...]

    a_row = -jnp.exp(al_ref[...])
    a_col = -jnp.exp(alt_ref[...])
    ii = lax.broadcasted_iota(jnp.int32, (SSD_CHUNK, SSD_CHUNK), 0)
    jj = lax.broadcasted_iota(jnp.int32, (SSD_CHUNK, SSD_CHUNK), 1)
    lane_lo = lax.broadcasted_iota(jnp.int32, (1, 2 * SSD_HEADDIM), 1) < SSD_HEADDIM
    for c in range(n_chunk):
        low = jj <= ii
        _ssd_chunk(0, c * SSD_CHUNK, xf_ref, bcf_ref, dtf_ref, dttf_ref, a_row, a_col, s_ref, yf_ref,
                   jnp.where(low, 1.0, 0.0), low, lane_lo)
        up = jj >= ii
        _ssd_chunk(1, (n_chunk - 1 - c) * SSD_CHUNK, xb_ref, bcb_ref, dtb_ref, dttb_ref, a_row, a_col, s_ref,
                   yb_ref, jnp.where(up, 1.0, 0.0), up, lane_lo)


def _ssd_scan(xs, bcf, bcb, dt, dtt, a_log_row, a_log_col, state0, batch, tb):
    n_rows, inner = xs.shape
    nbc = bcf.shape[1]
    nh2 = dt.shape[1]
    nb = n_rows // batch // tb
    fwd = lambda b, k: (b * nb + k, 0)
    bwd = lambda b, k: (b * nb + nb - 1 - k, 0)
    fwd_t = lambda b, k: (0, b * nb + k)
    bwd_t = lambda b, k: (0, b * nb + nb - 1 - k)
    s_spec = pl.BlockSpec((1, 2, SSD_STATE, inner), lambda b, k: (b, 0, 0, 0))
    kern = functools.partial(_ssd_scan_kernel, n_chunk=tb // SSD_CHUNK)
    return pl.pallas_call(
        kern,
        grid=(batch, nb),
        in_specs=[pl.BlockSpec((tb, inner), fwd), pl.BlockSpec((tb, inner), bwd),
                  pl.BlockSpec((tb, nbc), fwd), pl.BlockSpec((tb, nbc), bwd),
                  pl.BlockSpec((tb, nh2), fwd), pl.BlockSpec((tb, nh2), bwd),
                  pl.BlockSpec((nh2, tb), fwd_t), pl.BlockSpec((nh2, tb), bwd_t),
                  _full_spec(a_log_row.shape), _full_spec(a_log_col.shape), s_spec],
        out_specs=[pl.BlockSpec((tb, inner), fwd), pl.BlockSpec((tb, inner), bwd), s_spec],
        out_shape=[jax.ShapeDtypeStruct((n_rows, inner), BF16),
                   jax.ShapeDtypeStruct((n_rows, inner), BF16),
                   jax.ShapeDtypeStruct(state0.shape, F32)],
        compiler_params=_cparams("parallel", "arbitrary"),
        name="ssd_scan",
    )(xs, xs, bcf, bcb, dt, dt, dtt, dtt, a_log_row, a_log_col, state0)


def _ssd_gate(yf_ref, yb_ref, xs_ref, z_ref, d_ref, g_ref):
    y = yf_ref[...].astype(F32) + yb_ref[...].astype(F32) + d_ref[...] * xs_ref[...].astype(F32)
    y = y * _silu(z_ref[...].astype(F32))
    return _rms(y, g_ref[...]).astype(BF16)


def _qkv_kernel(x_ref, g_ref, sh_ref, sc_ref, w_ref, q_ref, k_ref, v_ref):
    d = x_ref.shape[1]
    hb = (_rms(x_ref[...], g_ref[...]) * (1.0 + sc_ref[0]) + sh_ref[0]).astype(BF16)
    q_ref[...] = (_dot(hb, w_ref[:, 0:d]) * (NA_HEADDIM ** -0.5)).astype(BF16)
    k_ref[...] = _dot(hb, w_ref[:, d:2 * d]).astype(BF16)
    v_ref[...] = _dot(hb, w_ref[:, 2 * d:3 * d]).astype(BF16)


def _qkv(x2, g0, shift, scale, group_rows, w, tm):
    n_rows, d = x2.shape
    x_spec, m_spec = _row_specs(tm, d, group_rows // tm)
    out = jax.ShapeDtypeStruct((n_rows, d), BF16)
    return pl.pallas_call(
        _qkv_kernel,
        grid=(n_rows // tm,),
        in_specs=[x_spec, _full_spec((1, d)), m_spec, m_spec, _full_spec(w.shape)],
        out_specs=[x_spec, x_spec, x_spec],
        out_shape=[out, out, out],
        compiler_params=_cparams("parallel"),
        name="na_qkv",
    )(x2, g0, shift, scale, w)


def _na_kernel(*refs):
    q_ref = refs[0]
    k_refs = refs[1:1 + NA_WIN_ROWS]
    v_refs = refs[1 + NA_WIN_ROWS:1 + 2 * NA_WIN_ROWS]
    kc_ref, vc_ref, bias_ref, o_ref = refs[1 + 2 * NA_WIN_ROWS:]
    r = pl.program_id(1)
    rows = pl.num_programs(1)
    ro0 = jnp.clip(r - NA_WIN_ROWS // 2, 0, rows - NA_WIN_ROWS) - r + NA_WIN_ROWS - 1
    lane_lo = lax.broadcasted_iota(jnp.int32, (1, 2 * NA_HEADDIM), 1) < NA_HEADDIM
    pair_lanes = [slice(p * 2 * NA_HEADDIM, (p + 1) * 2 * NA_HEADDIM) for p in range(NA_HEADS // 2)]
    nq = q_ref.shape[0]

    def qk(p):
        lanes = pair_lanes[p]
        q2 = q_ref[:, lanes]
        zero = jnp.zeros_like(q2)
        qs = jnp.concatenate([jnp.where(lane_lo, q2, zero), jnp.where(lane_lo, zero, q2)], axis=0)
        k2 = jnp.concatenate([kr[:, lanes] for kr in k_refs], axis=0)
        return _dot_nt(qs, k2), _dot_nt(qs, kc_ref[0, :, lanes])

    def softmax(p, s_loc, s_ctx):
        bias = jnp.concatenate(
            [jnp.concatenate([bias_ref[2 * p + hx, ro0 + 2 * w2] for w2 in range(NA_WIN_ROWS // 2)], axis=1)
             for hx in range(2)], axis=0)
        s_loc = s_loc + bias
        m = jnp.maximum(jnp.max(s_loc, axis=-1, keepdims=True), jnp.max(s_ctx, axis=-1, keepdims=True))
        p_loc = jnp.exp(s_loc - m)
        p_ctx = jnp.exp(s_ctx - m)
        den = jnp.sum(p_loc, axis=-1, keepdims=True) + jnp.sum(p_ctx, axis=-1, keepdims=True)
        return p_loc.astype(BF16), p_ctx.astype(BF16), den

    def pv(p, p_loc, p_ctx, den):
        lanes = pair_lanes[p]
        v2 = jnp.concatenate([vr[:, lanes] for vr in v_refs], axis=0)
        o = (_dot(p_loc, v2) + _dot(p_ctx, vc_ref[0, :, lanes])) / den
        o_ref[:, lanes] = jnp.where(lane_lo, o[:nq], o[nq:]).astype(BF16)

    n_pairs = len(pair_lanes)
    for p0 in range(0, n_pairs, NA_STAGE_PAIRS):
        group = range(p0, p0 + NA_STAGE_PAIRS)
        scores = [qk(p) for p in group]
        probs = [softmax(p, *s) for p, s in zip(group, scores)]
        for p, pr in zip(group, probs):
            pv(p, *pr)


def _na_bias_table(rpb):
    n_heads, n_ro, n_co = rpb.shape
    side = GRID_W - NA_WIN_COLS
    vp = jnp.pad(rpb, ((0, 0), (0, 0), (side, side + 1)), constant_values=NEG_BIG)
    width = n_co + 2 * side
    skew = jnp.broadcast_to(vp[:, :, None, :], (n_heads, n_ro, GRID_W, width + 1))
    skew = skew.reshape(n_heads, n_ro, GRID_W * (width + 1))[:, :, :GRID_W * width]
    toe = skew.reshape(n_heads, n_ro, GRID_W, width)[:, :, :, GRID_W - 1:]
    cols = np.arange(GRID_W)
    c0 = np.clip(cols - NA_WIN_COLS // 2, 0, GRID_W - NA_WIN_COLS)
    valid = (cols[None, :] >= c0[:, None]) & (cols[None, :] < c0[:, None] + NA_WIN_COLS)
    toe = jnp.where(valid[None, None], toe, NEG_BIG)
    return jnp.concatenate([toe[:, :-1], toe[:, 1:]], axis=-1)


def _na_attention(q, k, v, kc, vc, bias_tab, batch):
    n_rows, d = q.shape
    rows = n_rows // batch // GRID_W
    wr = NA_WIN_ROWS
    start = lambda r: jnp.clip(r - wr // 2, 0, rows - wr)
    q_spec = pl.BlockSpec((GRID_W, d), lambda b, r: (b * rows + r, 0))
    win = [pl.BlockSpec((GRID_W, d), functools.partial(lambda b, r, w: (b * rows + start(r) + w, 0), w=w))
           for w in range(wr)]
    c_spec = pl.BlockSpec((1,) + kc.shape[1:], lambda b, r: (b, 0, 0))
    return pl.pallas_call(
        _na_kernel,
        grid=(batch, rows),
        in_specs=[q_spec] + win + win + [c_spec, c_spec, _full_spec(bias_tab.shape)],
        out_specs=q_spec,
        out_shape=jax.ShapeDtypeStruct((n_rows, d), BF16),
        compiler_params=_cparams("parallel", "arbitrary"),
        name="na_attention",
    )(q, *([k] * wr), *([v] * wr), kc, vc, bias_tab)


def _sconv_kernel(xp_ref, x_ref, xn_ref, g_ref, sh_ref, sc_ref, w_ref, cw_ref, o_ref, *, tm, tiles_per_seq):
    d = x_ref.shape[1]
    h_all = _normed_with_halo(xp_ref, x_ref, xn_ref, g_ref, sh_ref, sc_ref, tiles_per_seq)
    hb_all = h_all.astype(BF16)
    hb = h_all[HALO:HALO + tm].astype(BF16)
    cw = 512
    n = tm + 2 * HALO
    for n0 in range(0, d, cw):
        c_gate = _dot(hb_all, w_ref[:, d + n0:d + n0 + cw])
        u = _dot(hb_all, w_ref[:, 2 * d + n0:2 * d + n0 + cw])
        v = c_gate * u
        prev = pltpu.roll(v, 1, 0)[HALO:HALO + tm]
        nxt = pltpu.roll(v, n - 1, 0)[HALO:HALO + tm]
        conv = (prev * cw_ref[0:1, n0:n0 + cw] + v[HALO:HALO + tm] * cw_ref[1:2, n0:n0 + cw]
                + nxt * cw_ref[2:3, n0:n0 + cw])
        b_gate = _dot(hb, w_ref[:, n0:n0 + cw])
        o_ref[:, n0:n0 + cw] = (b_gate * conv).astype(BF16)


def _sconv(x2, seq_rows, g0, shift, scale, w, conv_w, tm):
    n_rows, d = x2.shape
    x_spec, m_spec = _row_specs(tm, d, seq_rows // tm)
    prev, nxt = _halo_specs(tm, d, n_rows)
    kern = functools.partial(_sconv_kernel, tm=tm, tiles_per_seq=seq_rows // tm)
    return pl.pallas_call(
        kern,
        grid=(n_rows // tm,),
        in_specs=[prev, x_spec, nxt, _full_spec((1, d)), m_spec, m_spec, _full_spec(w.shape),
                  _full_spec(conv_w.shape)],
        out_specs=x_spec,
        out_shape=jax.ShapeDtypeStruct((n_rows, d), BF16),
        compiler_params=_cparams("parallel"),
        name="sconv",
    )(x2, x2, x2, g0, shift, scale, w, conv_w)


def _fourier1_kernel(x_ref, g_ref, sh_ref, sc_ref, cs_ref, km_ref, twc_ref, tws_ref, yr_ref, yi_ref):
    d = g_ref.shape[1]
    gc = d // FN_GROUPS
    n = DFT_ROWS * FN_SUB
    x = x_ref[0].reshape(n, d)
    hb = (_rms(x, g_ref[...]) * (1.0 + sc_ref[0]) + sh_ref[0]).astype(BF16)
    zr, zi = [], []
    for g in range(FN_GROUPS):
        zz = _dot(hb[:, g * gc:(g + 1) * gc], cs_ref[...])
        zr.append(zz[:, :gc])
        zi.append(zz[:, gc:])
    z = jnp.concatenate([jnp.concatenate(zr, axis=1), jnp.concatenate(zi, axis=1)], axis=0)
    y = _dot(km_ref[...], z.astype(BF16))
    yr, yi = y[:n], y[n:]
    tc, ts = twc_ref[0], tws_ref[0]
    yr_ref[0] = (yr * tc + yi * ts).reshape(FN_SUB, DFT_ROWS, d)
    yi_ref[0] = (yi * tc - yr * ts).reshape(FN_SUB, DFT_ROWS, d)


def _fourier2_kernel(yr_ref, yi_ref, gk_ref, o_ref):
    t2n, _, d = yr_ref.shape[1:]
    n = t2n * FN_SUB
    y = jnp.concatenate([yr_ref[0].reshape(n, d), yi_ref[0].reshape(n, d)], axis=0)
    o_ref[0] = _dot(gk_ref[...], y.astype(BF16)).reshape(t2n, FN_SUB, d)


def _dft_tables(seq, gc):
    t2n = seq // DFT_ROWS
    eye = np.eye(FN_SUB)
    c = np.arange(gc)
    ang_c = 2.0 * np.pi * np.outer(c, c) / gc
    cs = np.concatenate([np.cos(ang_c), -np.sin(ang_c)], axis=1)
    k1 = np.arange(DFT_ROWS)
    ang1 = 2.0 * np.pi * np.outer(k1, k1) / DFT_ROWS
    c1, s1 = np.cos(ang1), np.sin(ang1)
    m = np.stack([np.stack([c1, s1], axis=1), np.stack([-s1, c1], axis=1)], axis=0)
    km = np.einsum("akbt,sr->askbtr", m, eye).reshape(2 * DFT_ROWS * FN_SUB, 2 * DFT_ROWS * FN_SUB)
    t2 = np.arange(t2n)
    ang_tw = 2.0 * np.pi * np.outer(t2, k1) / seq
    tw_shape = (t2n // FN_SUB, FN_SUB * DFT_ROWS, 1)
    ang2 = 2.0 * np.pi * np.outer(t2, t2) / t2n
    g2 = np.stack([np.cos(ang2), np.sin(ang2)], axis=1)
    gk = np.einsum("kbt,sr->ksbtr", g2, eye).reshape(t2n * FN_SUB, 2 * t2n * FN_SUB)
    return (jnp.asarray(cs, BF16), jnp.asarray(km, BF16),
            jnp.asarray(np.cos(ang_tw).reshape(tw_shape), F32), jnp.asarray(np.sin(ang_tw).reshape(tw_shape), F32),
            jnp.asarray(gk, BF16))


def _fourier(x3, g0, shift, scale):
    batch, seq, d = x3.shape
    t2n = seq // DFT_ROWS
    n1 = DFT_ROWS * FN_SUB
    cs, km, twc, tws, gk = _dft_tables(seq, d // FN_GROUPS)
    xv = x3.reshape(batch, DFT_ROWS, t2n, d)
    m_spec = pl.BlockSpec((1, 1, d), lambda b, j: (b, 0, 0))
    y1_spec = pl.BlockSpec((1, FN_SUB, DFT_ROWS, d), lambda b, j: (b, j, 0, 0))
    tw_spec = pl.BlockSpec((1, n1, 1), lambda b, j: (j, 0, 0))
    y_shape = jax.ShapeDtypeStruct((batch, t2n, DFT_ROWS, d), F32)
    yr, yi = pl.pallas_call(
        _fourier1_kernel,
        grid=(batch, t2n // FN_SUB),
        in_specs=[pl.BlockSpec((1, DFT_ROWS, FN_SUB, d), lambda b, j: (b, 0, j, 0)),
                  _full_spec((1, d)), m_spec, m_spec, _full_spec(cs.shape), _full_spec(km.shape), tw_spec, tw_spec],
        out_specs=[y1_spec, y1_spec],
        out_shape=[y_shape, y_shape],
        compiler_params=_cparams("parallel", "parallel"),
        name="fourier_stage1",
    )(xv, g0, shift, scale, cs, km, twc, tws)
    y2_spec = pl.BlockSpec((1, t2n, FN_SUB, d), lambda b, j: (b, 0, j, 0))
    out = pl.pallas_call(
        _fourier2_kernel,
        grid=(batch, DFT_ROWS // FN_SUB),
        in_specs=[y2_spec, y2_spec, _full_spec(gk.shape)],
        out_specs=y2_spec,
        out_shape=y_shape,
        compiler_params=_cparams("parallel", "parallel"),
        name="fourier_stage2",
    )(yr, yi, gk)
    return out.reshape(batch * seq, d)


def _finish_kernel(x_ref, *refs, hc, n_mixer_in):
    mixer_in = refs[:n_mixer_in]
    wo_ref, m2_ref, m3_ref, m4_ref, m5_ref, g_ref, w1_ref, w2_ref, o_ref = refs[n_mixer_in:]
    z = _ssd_gate(*mixer_in) if n_mixer_in > 1 else mixer_in[0][...].astype(BF16)
    y = _dot(z, wo_ref[...])
    t = x_ref[...] + m2_ref[0] * _rms(y, g_ref[1:2, :])
    h2 = (_rms(t, g_ref[2:3, :]) * (1.0 + m4_ref[0]) + m3_ref[0]).astype(BF16)
    starts = list(range(0, w1_ref.shape[1], hc))
    up = _dot(h2, w1_ref[:, 0:hc])
    acc = None
    for n0 in starts:
        nxt = _dot(h2, w1_ref[:, n0 + hc:n0 + 2 * hc]) if n0 != starts[-1] else None
        a = jnp.maximum(up, 0.0)
        down = _dot((a * a).astype(BF16), w2_ref[n0:n0 + hc, :])
        acc = down if acc is None else acc + down
        up = nxt
    o_ref[...] = t + m5_ref[0] * _rms(acc, g_ref[3:4, :])


def _finish(x2, mixer_out, w_out, mods, g, w1, w2, group_rows, tm):
    n_rows, d = x2.shape
    x_spec, m_spec = _row_specs(tm, d, group_rows // tm)
    row_spec = lambda a: pl.BlockSpec((tm, a.shape[1]), lambda i: (i, 0))
    if isinstance(mixer_out, tuple):
        mixer_specs = [row_spec(a) for a in mixer_out[:4]] + [_full_spec(a.shape) for a in mixer_out[4:]]
    else:
        mixer_out = (mixer_out,)
        mixer_specs = [row_spec(mixer_out[0])]
    kern = functools.partial(_finish_kernel, hc=1024, n_mixer_in=len(mixer_out))
    return pl.pallas_call(
        kern,
        grid=(n_rows // tm,),
        in_specs=[x_spec] + mixer_specs + [_full_spec(w_out.shape), m_spec, m_spec, m_spec, m_spec,
                                           _full_spec(g.shape), _full_spec(w1.shape), _full_spec(w2.shape)],
        out_specs=x_spec,
        out_shape=jax.ShapeDtypeStruct((n_rows, d), F32),
        compiler_params=_cparams("parallel"),
        name="finish",
    )(x2, *mixer_out, w_out, mods[2], mods[3], mods[4], mods[5], g, w1, w2)


def _split_mods(m_rows, d):
    return [m_rows[:, k * d:(k + 1) * d][:, None, :] for k in range(N_MOD)]


def kernel(x, c, ctx, c_ctx, mod_w, mod_b, norm_g, mlp_w1, mlp_w2, ssd_w_in, ssd_conv_w, ssd_conv_b, ssd_dt_bias,
           ssd_a_log, ssd_d, ssd_norm_g, ssd_w_out, na_w_qkv, na_rpb, na_w_out, sc_w_in, sc_conv_w, sc_w_out,
           fn_w_out):
    batch, seq, d = x.shape
    ctx_len = ctx.shape[1]
    depth = mod_w.shape[0]
    n_lat = batch * seq
    n_ctx = batch * ctx_len
    tm = min(256, ctx_len)

    c_rows = jnp.concatenate([c, c_ctx[None, :], jnp.zeros((8 - batch - 1, d), F32)], axis=0)
    mods_all = _modulation(c_rows, mod_w, mod_b)

    xl = x.reshape(n_lat, d)
    xc = ctx.reshape(n_ctx, d)

    for i in range(depth):
        kind, j = i % 4, i // 4
        m_lat = _split_mods(mods_all[i, :batch], d)
        m_ctx = _split_mods(mods_all[i, batch:batch + 1], d)
        g = norm_g[i]
        g0 = g[0:1]
        w1 = mlp_w1[i].astype(BF16)
        w2 = mlp_w2[i].astype(BF16)
        ctx_later = any((l % 4) in (0, 1) for l in range(i + 1, depth))
        zc = None
        if kind == 0:
            inner = ssd_w_out.shape[1]
            nh = ssd_a_log.shape[2]
            w_in = ssd_w_in[j].astype(BF16)
            n_conv = ssd_conv_w.shape[2]
            pad_rows = 128 - 2 * nh
            w_dtt = jnp.pad(w_in[:, inner + n_conv:].T, ((0, pad_rows), (0, 0)))
            dt_bias_col = jnp.pad(ssd_dt_bias[j].reshape(2 * nh, 1), ((0, pad_rows), (0, 0)))
            prm = (w_in, w_dtt, ssd_conv_w[j], ssd_conv_b[j][None, :], dt_bias_col, inner, 2 * nh)
            a_row = ssd_a_log[j].reshape(1, 2 * nh)
            a_col = ssd_a_log[j].reshape(2 * nh, 1)
            d_vec = jnp.repeat(ssd_d[j].sum(0), SSD_HEADDIM)[None, :]
            ng = ssd_norm_g[j][None, :]
            w_out = ssd_w_out[j].astype(BF16)
            tb = min(256, ctx_len)

            def mixer(rows2, seq_rows, group_rows, shift, scale, state0):
                z, xs, bcf, bcb, dt, dtt = _ssd_inproj(rows2, seq_rows, g0, shift, scale, group_rows, prm, tm)
                yf, yb, s_fin = _ssd_scan(xs, bcf, bcb, dt, dtt, a_row, a_col, state0, batch, tb)
                return (yf, yb, xs, z, d_vec, ng), s_fin

            zero_state = jnp.zeros((batch, 2, SSD_STATE, inner), F32)
            zc, ctx_state = mixer(xc, ctx_len, n_ctx, m_ctx[0], m_ctx[1], zero_state)
            zl, _ = mixer(xl, seq, seq, m_lat[0], m_lat[1], ctx_state)
        elif kind == 1:
            w_qkv = na_w_qkv[j].astype(BF16)
            w_out = na_w_out[j].astype(BF16)
            qc, kc, vc = _qkv(xc, g0, m_ctx[0], m_ctx[1], n_ctx, w_qkv, tm)
            q, k, v = _qkv(xl, g0, m_lat[0], m_lat[1], seq, w_qkv, tm)
            kc = kc.reshape(batch, ctx_len, d)
            vc = vc.reshape(batch, ctx_len, d)
            zl = _na_attention(q, k, v, kc, vc, _na_bias_table(na_rpb[j]), batch)
            if ctx_later:
                raise NotImplementedError("context self-attention is only needed when a later layer reads ctx")
        elif kind == 2:
            w_out = sc_w_out[j].astype(BF16)
            zl = _sconv(xl, seq, g0, m_lat[0], m_lat[1], sc_w_in[j].astype(BF16), sc_conv_w[j], tm)
            if ctx_later:
                zc = _sconv(xc, ctx_len, g0, m_ctx[0], m_ctx[1], sc_w_in[j].astype(BF16), sc_conv_w[j], tm)
        else:
            w_out = fn_w_out[j].astype(BF16)
            zl = _fourier(xl.reshape(batch, seq, d), g0, m_lat[0], m_lat[1])
            if ctx_later:
                raise NotImplementedError("context Fourier mixing is only needed when a later layer reads ctx")
        xl = _finish(xl, zl, w_out, m_lat, g, w1, w2, seq, tm)
        if ctx_later:
            xc = _finish(xc, zc, w_out, m_ctx, g, w1, w2, n_ctx, tm)
    return xl.reshape(batch, seq, d)
```

```python
import functools
import math

import jax
import jax.numpy as jnp
import numpy as np
from jax import lax
from jax.experimental import pallas as pl
from jax.experimental.pallas import tpu as pltpu

F32 = jnp.float32
BF16 = jnp.bfloat16
HIGHEST = lax.Precision.HIGHEST

RMS_EPS = 1e-6
N_MOD = 6
GRID_W = 64
HALO = 8
SSD_CHUNK = 128
SSD_HEADDIM = 64
SSD_STATE = 128
SSD_GROUPS = 4
NA_HEADS = 16
NA_HEADDIM = 64
NA_WIN_ROWS = 8
NA_WIN_COLS = 16
NA_STAGE_PAIRS = 4
FN_GROUPS = 8
DFT_ROWS = 64
FN_SUB = 8
NEG_BIG = -1e30
LOG2_E = math.log2(math.e)
VMEM_LIMIT = 56 * 1024 * 1024
FINISH_SUB = 256


def _cparams(*sem):
    return pltpu.CompilerParams(dimension_semantics=sem, vmem_limit_bytes=VMEM_LIMIT)


def _full_spec(shape):
    nd = len(shape)
    return pl.BlockSpec(shape, lambda *_: (0,) * nd)


def _rms(t, g):
    return t * lax.rsqrt(jnp.mean(t * t, axis=-1, keepdims=True) + RMS_EPS) * g


def _silu(t):
    half = 0.5 * t
    return half + half * jnp.tanh(half)


def _softplus(t):
    return jnp.maximum(t, 0.0) + jnp.log1p(jnp.exp(-jnp.abs(t)))


def _dot(a, b, **kw):
    return jnp.dot(a, b, preferred_element_type=F32, **kw)


def _dot_nt(a, b, **kw):
    return lax.dot_general(a, b, (((1,), (1,)), ((), ())), preferred_element_type=F32, **kw)


def _mod_kernel(c_ref, w_ref, b_ref, o_ref):
    act = _silu(c_ref[...]).astype(BF16)
    o_ref[0] = _dot(act, w_ref[0].astype(BF16)) + b_ref[0]


def _modulation(c_rows, mod_w, mod_b):
    depth, d, n = mod_w.shape
    tn = n // 4
    rows = c_rows.shape[0]
    return pl.pallas_call(
        _mod_kernel,
        grid=(depth, n // tn),
        in_specs=[
            pl.BlockSpec((rows, d), lambda i, j: (0, 0)),
            pl.BlockSpec((1, d, tn), lambda i, j: (i, 0, j)),
            pl.BlockSpec((1, 1, tn), lambda i, j: (i, 0, j)),
        ],
        out_specs=pl.BlockSpec((1, rows, tn), lambda i, j: (i, 0, j)),
        out_shape=jax.ShapeDtypeStruct((depth, rows, n), F32),
        compiler_params=_cparams("parallel", "parallel"),
        name="modulation",
    )(c_rows, mod_w, mod_b.reshape(depth, 1, n))


def _row_specs(tm, d, tiles_per_group):
    x_spec = pl.BlockSpec((tm, d), lambda i: (i, 0))
    m_spec = pl.BlockSpec((1, 1, d), lambda i: (i // tiles_per_group, 0, 0))
    return x_spec, m_spec


def _halo_specs(tm, d, n_rows):
    per = tm // HALO
    last = n_rows // HALO - 1
    prev = pl.BlockSpec((HALO, d), lambda i: (jnp.maximum(i * per - 1, 0), 0))
    nxt = pl.BlockSpec((HALO, d), lambda i: (jnp.minimum((i + 1) * per, last), 0))
    return prev, nxt


def _normed_with_halo(xp_ref, x_ref, xn_ref, g_ref, sh_ref, sc_ref, tiles_per_seq):
    i = pl.program_id(0) % tiles_per_seq
    norm = lambda ref: _rms(ref[...], g_ref[...]) * (1.0 + sc_ref[0]) + sh_ref[0]
    keep_prev = jnp.where(i == 0, 0.0, 1.0)
    keep_next = jnp.where(i == tiles_per_seq - 1, 0.0, 1.0)
    return jnp.concatenate([norm(xp_ref) * keep_prev, norm(x_ref), norm(xn_ref) * keep_next], axis=0)


def _conv3(v_ref, w_ref, tm):
    prev = v_ref[pl.ds(HALO - 1, tm), :]
    cur = v_ref[pl.ds(HALO, tm), :]
    nxt = v_ref[pl.ds(HALO + 1, tm), :]
    return prev * w_ref[0:1, :] + cur * w_ref[1:2, :] + nxt * w_ref[2:3, :]


SSD_CONV_TILE = 1024


def _ssd_inproj_kernel(xp_ref, x_ref, xn_ref, g_ref, sh_ref, sc_ref, w_ref, wdtt_ref, cw_ref, cb_ref, dtbt_ref,
                       z_ref, xs_ref, bcf_ref, bcb_ref, dt_ref, dtt_ref, raw_ref, *, tm, tiles_per_seq, inner):
    h_all = _normed_with_halo(xp_ref, x_ref, xn_ref, g_ref, sh_ref, sc_ref, tiles_per_seq)
    hb_all = h_all.astype(BF16)
    hb = h_all[HALO:HALO + tm].astype(BF16)

    z_ref[...] = _dot(hb, w_ref[:, 0:inner]).astype(BF16)
    nh2 = dtt_ref.shape[0]
    dtt = _softplus(_dot_nt(wdtt_ref[...], hb) + dtbt_ref[...])
    dtt_ref[...] = dtt[:nh2]
    dt_ref[...] = dtt.T[:, :nh2]

    cw = SSD_CONV_TILE
    for n0 in range(0, cw_ref.shape[1], cw):
        raw_ref[...] = _dot(hb_all, w_ref[:, inner + n0:inner + n0 + cw])
        act = _silu(_conv3(raw_ref, cw_ref.at[:, n0:n0 + cw], tm) + cb_ref[:, n0:n0 + cw]).astype(BF16)
        if n0 < inner:
            xs_ref[:, n0:n0 + cw] = act
        elif n0 < inner + cw:
            bcf_ref[...] = act
        else:
            bcb_ref[...] = act


def _ssd_inproj(x2, n_seq_rows, g0, shift, scale, tiles_per_group_rows, prm, tm):
    n_rows, d = x2.shape
    w_in, wdtt, conv_w, conv_b, dtbt, inner, nh2 = prm
    nbc = (conv_w.shape[1] - inner) // 2
    assert nbc == SSD_CONV_TILE and inner % SSD_CONV_TILE == 0
    x_spec, m_spec = _row_specs(tm, d, tiles_per_group_rows // tm)
    prev, nxt = _halo_specs(tm, d, n_rows)
    kern = functools.partial(_ssd_inproj_kernel, tm=tm, tiles_per_seq=n_seq_rows // tm, inner=inner)
    row_spec = lambda n: pl.BlockSpec((tm, n), lambda i: (i, 0))
    return pl.pallas_call(
        kern,
        grid=(n_rows // tm,),
        in_specs=[prev, x_spec, nxt, _full_spec((1, d)), m_spec, m_spec, _full_spec(w_in.shape),
                  _full_spec(wdtt.shape), _full_spec(conv_w.shape), _full_spec(conv_b.shape), _full_spec(dtbt.shape)],
        out_specs=[row_spec(inner), row_spec(inner), row_spec(nbc), row_spec(nbc), row_spec(nh2),
                   pl.BlockSpec((nh2, tm), lambda i: (0, i))],
        out_shape=[jax.ShapeDtypeStruct((n_rows, inner), BF16),
                   jax.ShapeDtypeStruct((n_rows, inner), BF16),
                   jax.ShapeDtypeStruct((n_rows, nbc), BF16),
                   jax.ShapeDtypeStruct((n_rows, nbc), BF16),
                   jax.ShapeDtypeStruct((n_rows, nh2), F32),
                   jax.ShapeDtypeStruct((nh2, n_rows), F32)],
        scratch_shapes=[pltpu.VMEM((tm + 2 * HALO, SSD_CONV_TILE), F32)],
        compiler_params=_cparams("parallel"),
        name="ssd_inproj",
    )(x2, x2, x2, g0, shift, scale, w_in, wdtt, conv_w, conv_b, dtbt)


def _ssd_chunk(d, r0, x_ref, bc_ref, dt_ref, dtt_ref, a_row, a_col, s_ref, y_ref, tri, mask, lane_lo):
    nh = a_row.shape[1] // 2
    hpg = nh // SSD_GROUPS
    rows = pl.ds(r0, SSD_CHUNK)
    dt = dt_ref[rows, :]
    dtt = dtt_ref[:, rows]
    cum = _dot(tri, dt * a_row, precision=HIGHEST) * LOG2_E
    cumt = _dot_nt(dtt * a_col, tri, precision=HIGHEST) * LOG2_E
    end = SSD_CHUNK - 1 if d == 0 else 0
    tot = cum[end:end + 1, :]
    tott = cumt[:, end:end + 1]
    wt = jnp.exp2(tott - cumt) * dtt
    dec_s = jnp.exp2(tot)
    cumt_dt = cumt - jnp.log2(dtt)
    for g in range(SSD_GROUPS):
        bg = bc_ref[rows, g * SSD_STATE:(g + 1) * SSD_STATE]
        cg = bc_ref[rows, (SSD_GROUPS + g) * SSD_STATE:(SSD_GROUPS + g + 1) * SSD_STATE]
        cb = _dot_nt(cg, bg)
        bt = bg.astype(F32).T
        cgf = cg.astype(F32)
        for p in range(hpg // 2):
            l0 = (g * hpg + 2 * p) * SSD_HEADDIM
            x2b = x_ref[rows, l0:l0 + 2 * SSD_HEADDIM]
            s2 = s_ref[0, d, :, l0:l0 + 2 * SSD_HEADDIM]
            rhs = jnp.concatenate([x2b, s2.astype(BF16)], axis=0)
            lhs_out, lhs_up, decs = [], [], []
            for hx in range(2):
                col = d * nh + g * hpg + 2 * p + hx
                colb = jnp.broadcast_to(cum[:, col:col + 1], (SSD_CHUNK, SSD_CHUNK))
                rowb = cumt_dt[col:col + 1, :]
                sc = (cb * jnp.exp2(jnp.where(mask, colb - rowb, NEG_BIG))).astype(BF16)
                co = (cgf * jnp.exp2(colb)).astype(BF16)
                lhs_out.append(jnp.concatenate([sc, co], axis=1))
                lhs_up.append((bt * wt[col:col + 1, :]).astype(BF16))
                decs.append(dec_s[:, col:col + 1])
            out2 = _dot(jnp.concatenate(lhs_out, axis=0), rhs)
            up2 = _dot(jnp.concatenate(lhs_up, axis=0), x2b)
            y_ref[rows, l0:l0 + 2 * SSD_HEADDIM] = jnp.where(
                lane_lo, out2[:SSD_CHUNK], out2[SSD_CHUNK:]).astype(BF16)
            s_ref[0, d, :, l0:l0 + 2 * SSD_HEADDIM] = (
                s2 * jnp.where(lane_lo, decs[0], decs[1])
                + jnp.where(lane_lo, up2[:SSD_STATE], up2[SSD_STATE:]))


def _ssd_scan_kernel(xf_ref, xb_ref, bcf_ref, bcb_ref, dtf_ref, dtb_ref, dttf_ref, dttb_ref,
                     al_ref, alt_ref, s0_ref, yf_ref, yb_ref, s_ref, *, n_chunk):
    @pl.when(pl.program_id(1) == 0)
    def _():
        s_ref[...] = s0_ref[...]

    a_row = -jnp.exp(al_ref[...])
    a_col = -jnp.exp(alt_ref[...])
    ii = lax.broadcasted_iota(jnp.int32, (SSD_CHUNK, SSD_CHUNK), 0)
    jj = lax.broadcasted_iota(jnp.int32, (SSD_CHUNK, SSD_CHUNK), 1)
    lane_lo = lax.broadcasted_iota(jnp.int32, (1, 2 * SSD_HEADDIM), 1) < SSD_HEADDIM
    for c in range(n_chunk):
        low = jj <= ii
        _ssd_chunk(0, c * SSD_CHUNK, xf_ref, bcf_ref, dtf_ref, dttf_ref, a_row, a_col, s_ref, yf_ref,
                   jnp.where(low, 1.0, 0.0), low, lane_lo)
        up = jj >= ii
        _ssd_chunk(1, (n_chunk - 1 - c) * SSD_CHUNK, xb_ref, bcb_ref, dtb_ref, dttb_ref, a_row, a_col, s_ref,
                   yb_ref, jnp.where(up, 1.0, 0.0), up, lane_lo)


def _ssd_scan(xs, bcf, bcb, dt, dtt, a_log_row, a_log_col, state0, batch, tb):
    n_rows, inner = xs.shape
    nbc = bcf.shape[1]
    nh2 = dt.shape[1]
    nb = n_rows // batch // tb
    fwd = lambda b, k: (b * nb + k, 0)
    bwd = lambda b, k: (b * nb + nb - 1 - k, 0)
    fwd_t = lambda b, k: (0, b * nb + k)
    bwd_t = lambda b, k: (0, b * nb + nb - 1 - k)
    s_spec = pl.BlockSpec((1, 2, SSD_STATE, inner), lambda b, k: (b, 0, 0, 0))
    kern = functools.partial(_ssd_scan_kernel, n_chunk=tb // SSD_CHUNK)
    return pl.pallas_call(
        kern,
        grid=(batch, nb),
        in_specs=[pl.BlockSpec((tb, inner), fwd), pl.BlockSpec((tb, inner), bwd),
                  pl.BlockSpec((tb, nbc), fwd), pl.BlockSpec((tb, nbc), bwd),
                  pl.BlockSpec((tb, nh2), fwd), pl.BlockSpec((tb, nh2), bwd),
                  pl.BlockSpec((nh2, tb), fwd_t), pl.BlockSpec((nh2, tb), bwd_t),
                  _full_spec(a_log_row.shape), _full_spec(a_log_col.shape), s_spec],
        out_specs=[pl.BlockSpec((tb, inner), fwd), pl.BlockSpec((tb, inner), bwd), s_spec],
        out_shape=[jax.ShapeDtypeStruct((n_rows, inner), BF16),
                   jax.ShapeDtypeStruct((n_rows, inner), BF16),
                   jax.ShapeDtypeStruct(state0.shape, F32)],
        compiler_params=_cparams("parallel", "arbitrary"),
        name="ssd_scan",
    )(xs, xs, bcf, bcb, dt, dt, dtt, dtt, a_log_row, a_log_col, state0)


def _ssd_gate(yf_ref, yb_ref, xs_ref, z_ref, d_ref, g_ref):
    y = yf_ref[...].astype(F32) + yb_ref[...].astype(F32) + d_ref[...] * xs_ref[...].astype(F32)
    y = y * _silu(z_ref[...].astype(F32))
    return _rms(y, g_ref[...]).astype(BF16)


def _qkv_kernel(x_ref, g_ref, sh_ref, sc_ref, w_ref, q_ref, k_ref, v_ref):
    d = x_ref.shape[1]
    for r0 in range(0, x_ref.shape[0], FINISH_SUB):
        rows = pl.ds(r0, FINISH_SUB)
        hb = (_rms(x_ref[rows, :], g_ref[...]) * (1.0 + sc_ref[0]) + sh_ref[0]).astype(BF16)
        q_ref[rows, :] = (_dot(hb, w_ref[:, 0:d]) * (NA_HEADDIM ** -0.5)).astype(BF16)
        k_ref[rows, :] = _dot(hb, w_ref[:, d:2 * d]).astype(BF16)
        v_ref[rows, :] = _dot(hb, w_ref[:, 2 * d:3 * d]).astype(BF16)


def _qkv(x2, g0, shift, scale, group_rows, w, tm):
    n_rows, d = x2.shape
    x_spec, m_spec = _row_specs(tm, d, group_rows // tm)
    out = jax.ShapeDtypeStruct((n_rows, d), BF16)
    return pl.pallas_call(
        _qkv_kernel,
        grid=(n_rows // tm,),
        in_specs=[x_spec, _full_spec((1, d)), m_spec, m_spec, _full_spec(w.shape)],
        out_specs=[x_spec, x_spec, x_spec],
        out_shape=[out, out, out],
        compiler_params=_cparams("parallel"),
        name="na_qkv",
    )(x2, g0, shift, scale, w)


def _na_kernel(q_ref, k_ref, v_ref, kc_ref, vc_ref, bias_ref, o_ref):
    r = pl.program_id(1)
    rows = pl.num_programs(1)
    ro0 = jnp.clip(r - NA_WIN_ROWS // 2, 0, rows - NA_WIN_ROWS) - r + NA_WIN_ROWS - 1
    lane_lo = lax.broadcasted_iota(jnp.int32, (1, 2 * NA_HEADDIM), 1) < NA_HEADDIM
    pair_lanes = [slice(p * 2 * NA_HEADDIM, (p + 1) * 2 * NA_HEADDIM) for p in range(NA_HEADS // 2)]
    nq = q_ref.shape[0]

    def qk(p):
        lanes = pair_lanes[p]
        q2 = q_ref[:, lanes]
        zero = jnp.zeros_like(q2)
        qs = jnp.concatenate([jnp.where(lane_lo, q2, zero), jnp.where(lane_lo, zero, q2)], axis=0)
        return _dot_nt(qs, k_ref[:, lanes]), _dot_nt(qs, kc_ref[0, :, lanes])

    def softmax(p, s_loc, s_ctx):
        bias = jnp.concatenate(
            [jnp.concatenate([bias_ref[2 * p + hx, ro0 + 2 * w2] for w2 in range(NA_WIN_ROWS // 2)], axis=1)
             for hx in range(2)], axis=0)
        s_loc = s_loc + bias
        m = jnp.maximum(jnp.max(s_loc, axis=-1, keepdims=True), jnp.max(s_ctx, axis=-1, keepdims=True))
        p_loc = jnp.exp(s_loc - m)
        p_ctx = jnp.exp(s_ctx - m)
        den = jnp.sum(p_loc, axis=-1, keepdims=True) + jnp.sum(p_ctx, axis=-1, keepdims=True)
        return p_loc.astype(BF16), p_ctx.astype(BF16), den

    def pv(p, p_loc, p_ctx, den):
        lanes = pair_lanes[p]
        o = (_dot(p_loc, v_ref[:, lanes]) + _dot(p_ctx, vc_ref[0, :, lanes])) / den
        o_ref[:, lanes] = jnp.where(lane_lo, o[:nq], o[nq:]).astype(BF16)

    n_pairs = len(pair_lanes)
    for p0 in range(0, n_pairs, NA_STAGE_PAIRS):
        group = range(p0, p0 + NA_STAGE_PAIRS)
        scores = [qk(p) for p in group]
        probs = [softmax(p, *s) for p, s in zip(group, scores)]
        for p, pr in zip(group, probs):
            pv(p, *pr)


def _na_bias_table(rpb):
    n_heads, n_ro, n_co = rpb.shape
    side = GRID_W - NA_WIN_COLS
    vp = jnp.pad(rpb, ((0, 0), (0, 0), (side, side + 1)), constant_values=NEG_BIG)
    width = n_co + 2 * side
    skew = jnp.broadcast_to(vp[:, :, None, :], (n_heads, n_ro, GRID_W, width + 1))
    skew = skew.reshape(n_heads, n_ro, GRID_W * (width + 1))[:, :, :GRID_W * width]
    toe = skew.reshape(n_heads, n_ro, GRID_W, width)[:, :, :, GRID_W - 1:]
    cols = np.arange(GRID_W)
    c0 = np.clip(cols - NA_WIN_COLS // 2, 0, GRID_W - NA_WIN_COLS)
    valid = (cols[None, :] >= c0[:, None]) & (cols[None, :] < c0[:, None] + NA_WIN_COLS)
    toe = jnp.where(valid[None, None], toe, NEG_BIG)
    return jnp.concatenate([toe[:, :-1], toe[:, 1:]], axis=-1)


def _na_attention(q, k, v, kc, vc, bias_tab, batch):
    n_rows, d = q.shape
    rows = n_rows // batch // GRID_W
    wr = NA_WIN_ROWS
    start = lambda r: jnp.clip(r - wr // 2, 0, rows - wr)
    q_spec = pl.BlockSpec((GRID_W, d), lambda b, r: (b * rows + r, 0))
    win = pl.BlockSpec((pl.Element(wr * GRID_W), pl.Element(d)),
                       lambda b, r: ((b * rows + start(r)) * GRID_W, 0))
    c_spec = pl.BlockSpec((1,) + kc.shape[1:], lambda b, r: (b, 0, 0))
    return pl.pallas_call(
        _na_kernel,
        grid=(batch, rows),
        in_specs=[q_spec, win, win, c_spec, c_spec, _full_spec(bias_tab.shape)],
        out_specs=q_spec,
        out_shape=jax.ShapeDtypeStruct((n_rows, d), BF16),
        compiler_params=_cparams("parallel", "arbitrary"),
        name="na_attention",
    )(q, k, v, kc, vc, bias_tab)


def _sconv_kernel(xp_ref, x_ref, xn_ref, g_ref, sh_ref, sc_ref, w_ref, cw_ref, o_ref, *, tm, tiles_per_seq):
    d = x_ref.shape[1]
    h_all = _normed_with_halo(xp_ref, x_ref, xn_ref, g_ref, sh_ref, sc_ref, tiles_per_seq)
    hb_all = h_all.astype(BF16)
    hb = h_all[HALO:HALO + tm].astype(BF16)
    cw = 512
    n = tm + 2 * HALO
    for n0 in range(0, d, cw):
        c_gate = _dot(hb_all, w_ref[:, d + n0:d + n0 + cw])
        u = _dot(hb_all, w_ref[:, 2 * d + n0:2 * d + n0 + cw])
        v = c_gate * u
        prev = pltpu.roll(v, 1, 0)[HALO:HALO + tm]
        nxt = pltpu.roll(v, n - 1, 0)[HALO:HALO + tm]
        conv = (prev * cw_ref[0:1, n0:n0 + cw] + v[HALO:HALO + tm] * cw_ref[1:2, n0:n0 + cw]
                + nxt * cw_ref[2:3, n0:n0 + cw])
        b_gate = _dot(hb, w_ref[:, n0:n0 + cw])
        o_ref[:, n0:n0 + cw] = (b_gate * conv).astype(BF16)


def _sconv(x2, seq_rows, g0, shift, scale, w, conv_w, tm):
    n_rows, d = x2.shape
    x_spec, m_spec = _row_specs(tm, d, seq_rows // tm)
    prev, nxt = _halo_specs(tm, d, n_rows)
    kern = functools.partial(_sconv_kernel, tm=tm, tiles_per_seq=seq_rows // tm)
    return pl.pallas_call(
        kern,
        grid=(n_rows // tm,),
        in_specs=[prev, x_spec, nxt, _full_spec((1, d)), m_spec, m_spec, _full_spec(w.shape),
                  _full_spec(conv_w.shape)],
        out_specs=x_spec,
        out_shape=jax.ShapeDtypeStruct((n_rows, d), BF16),
        compiler_params=_cparams("parallel"),
        name="sconv",
    )(x2, x2, x2, g0, shift, scale, w, conv_w)


def _fourier1_kernel(x_ref, g_ref, sh_ref, sc_ref, cs_ref, km_ref, twc_ref, tws_ref, yr_ref, yi_ref):
    d = g_ref.shape[1]
    gc = d // FN_GROUPS
    n = DFT_ROWS * FN_SUB
    x = x_ref[0].reshape(n, d)
    hb = (_rms(x, g_ref[...]) * (1.0 + sc_ref[0]) + sh_ref[0]).astype(BF16)
    zr, zi = [], []
    for g in range(FN_GROUPS):
        zz = _dot(hb[:, g * gc:(g + 1) * gc], cs_ref[...])
        zr.append(zz[:, :gc])
        zi.append(zz[:, gc:])
    z = jnp.concatenate([jnp.concatenate(zr, axis=1), jnp.concatenate(zi, axis=1)], axis=0)
    y = _dot(km_ref[...], z.astype(BF16))
    yr, yi = y[:n], y[n:]
    tc, ts = twc_ref[0], tws_ref[0]
    yr_ref[0] = (yr * tc + yi * ts).reshape(FN_SUB, DFT_ROWS, d)
    yi_ref[0] = (yi * tc - yr * ts).reshape(FN_SUB, DFT_ROWS, d)


def _fourier2_kernel(yr_ref, yi_ref, gk_ref, o_ref):
    t2n, _, d = yr_ref.shape[1:]
    n = t2n * FN_SUB
    y = jnp.concatenate([yr_ref[0].reshape(n, d), yi_ref[0].reshape(n, d)], axis=0)
    o_ref[0] = _dot(gk_ref[...], y.astype(BF16)).reshape(t2n, FN_SUB, d)


def _dft_tables(seq, gc):
    t2n = seq // DFT_ROWS
    eye = np.eye(FN_SUB)
    c = np.arange(gc)
    ang_c = 2.0 * np.pi * np.outer(c, c) / gc
    cs = np.concatenate([np.cos(ang_c), -np.sin(ang_c)], axis=1)
    k1 = np.arange(DFT_ROWS)
    ang1 = 2.0 * np.pi * np.outer(k1, k1) / DFT_ROWS
    c1, s1 = np.cos(ang1), np.sin(ang1)
    m = np.stack([np.stack([c1, s1], axis=1), np.stack([-s1, c1], axis=1)], axis=0)
    km = np.einsum("akbt,sr->askbtr", m, eye).reshape(2 * DFT_ROWS * FN_SUB, 2 * DFT_ROWS * FN_SUB)
    t2 = np.arange(t2n)
    ang_tw = 2.0 * np.pi * np.outer(t2, k1) / seq
    tw_shape = (t2n // FN_SUB, FN_SUB * DFT_ROWS, 1)
    ang2 = 2.0 * np.pi * np.outer(t2, t2) / t2n
    g2 = np.stack([np.cos(ang2), np.sin(ang2)], axis=1)
    gk = np.einsum("kbt,sr->ksbtr", g2, eye).reshape(t2n * FN_SUB, 2 * t2n * FN_SUB)
    return (jnp.asarray(cs, BF16), jnp.asarray(km, BF16),
            jnp.asarray(np.cos(ang_tw).reshape(tw_shape), F32), jnp.asarray(np.sin(ang_tw).reshape(tw_shape), F32),
            jnp.asarray(gk, BF16))


def _fourier(x3, g0, shift, scale):
    batch, seq, d = x3.shape
    t2n = seq // DFT_ROWS
    n1 = DFT_ROWS * FN_SUB
    cs, km, twc, tws, gk = _dft_tables(seq, d // FN_GROUPS)
    xv = x3.reshape(batch, DFT_ROWS, t2n, d)
    m_spec = pl.BlockSpec((1, 1, d), lambda b, j: (b, 0, 0))
    y1_spec = pl.BlockSpec((1, FN_SUB, DFT_ROWS, d), lambda b, j: (b, j, 0, 0))
    tw_spec = pl.BlockSpec((1, n1, 1), lambda b, j: (j, 0, 0))
    y_shape = jax.ShapeDtypeStruct((batch, t2n, DFT_ROWS, d), F32)
    yr, yi = pl.pallas_call(
        _fourier1_kernel,
        grid=(batch, t2n // FN_SUB),
        in_specs=[pl.BlockSpec((1, DFT_ROWS, FN_SUB, d), lambda b, j: (b, 0, j, 0)),
                  _full_spec((1, d)), m_spec, m_spec, _full_spec(cs.shape), _full_spec(km.shape), tw_spec, tw_spec],
        out_specs=[y1_spec, y1_spec],
        out_shape=[y_shape, y_shape],
        compiler_params=_cparams("parallel", "parallel"),
        name="fourier_stage1",
    )(xv, g0, shift, scale, cs, km, twc, tws)
    y2_spec = pl.BlockSpec((1, t2n, FN_SUB, d), lambda b, j: (b, 0, j, 0))
    out = pl.pallas_call(
        _fourier2_kernel,
        grid=(batch, DFT_ROWS // FN_SUB),
        in_specs=[y2_spec, y2_spec, _full_spec(gk.shape)],
        out_specs=y2_spec,
        out_shape=y_shape,
        compiler_params=_cparams("parallel", "parallel"),
        name="fourier_stage2",
    )(yr, yi, gk)
    return out.reshape(batch * seq, d)


def _finish_kernel(x_ref, *refs, hc, n_mixer_in):
    mixer_in = refs[:n_mixer_in]
    wo_ref, m2_ref, m3_ref, m4_ref, m5_ref, g_ref, w1_ref, w2_ref, o_ref = refs[n_mixer_in:]
    subs = [pl.ds(r0, FINISH_SUB) for r0 in range(0, x_ref.shape[0], FINISH_SUB)]
    ys = []
    for rows in subs:
        if n_mixer_in > 1:
            z = _ssd_gate(*[ref.at[rows, :] for ref in mixer_in[:4]], *mixer_in[4:])
        else:
            z = mixer_in[0][rows, :].astype(BF16)
        ys.append(_dot(z, wo_ref[...]))
    for rows, y in zip(subs, ys):
        t = x_ref[rows, :] + m2_ref[0] * _rms(y, g_ref[1:2, :])
        h2 = (_rms(t, g_ref[2:3, :]) * (1.0 + m4_ref[0]) + m3_ref[0]).astype(BF16)
        acc = None
        for n0 in range(0, w1_ref.shape[1], hc):
            a = jnp.maximum(_dot(h2, w1_ref[:, n0:n0 + hc]), 0.0)
            down = _dot((a * a).astype(BF16), w2_ref[n0:n0 + hc, :])
            acc = down if acc is None else acc + down
        o_ref[rows, :] = t + m5_ref[0] * _rms(acc, g_ref[3:4, :])


def _finish(x2, mixer_out, w_out, mods, g, w1, w2, group_rows, tm):
    n_rows, d = x2.shape
    x_spec, m_spec = _row_specs(tm, d, group_rows // tm)
    row_spec = lambda a: pl.BlockSpec((tm, a.shape[1]), lambda i: (i, 0))
    if isinstance(mixer_out, tuple):
        mixer_specs = [row_spec(a) for a in mixer_out[:4]] + [_full_spec(a.shape) for a in mixer_out[4:]]
    else:
        mixer_out = (mixer_out,)
        mixer_specs = [row_spec(mixer_out[0])]
    kern = functools.partial(_finish_kernel, hc=1024, n_mixer_in=len(mixer_out))
    return pl.pallas_call(
        kern,
        grid=(n_rows // tm,),
        in_specs=[x_spec] + mixer_specs + [_full_spec(w_out.shape), m_spec, m_spec, m_spec, m_spec,
                                           _full_spec(g.shape), _full_spec(w1.shape), _full_spec(w2.shape)],
        out_specs=x_spec,
        out_shape=jax.ShapeDtypeStruct((n_rows, d), F32),
        compiler_params=_cparams("parallel"),
        name="finish",
    )(x2, *mixer_out, w_out, mods[2], mods[3], mods[4], mods[5], g, w1, w2)


def _split_mods(m_rows, d):
    return [m_rows[:, k * d:(k + 1) * d][:, None, :] for k in range(N_MOD)]


def kernel(x, c, ctx, c_ctx, mod_w, mod_b, norm_g, mlp_w1, mlp_w2, ssd_w_in, ssd_conv_w, ssd_conv_b, ssd_dt_bias,
           ssd_a_log, ssd_d, ssd_norm_g, ssd_w_out, na_w_qkv, na_rpb, na_w_out, sc_w_in, sc_conv_w, sc_w_out,
           fn_w_out):
    batch, seq, d = x.shape
    ctx_len = ctx.shape[1]
    depth = mod_w.shape[0]
    n_lat = batch * seq
    n_ctx = batch * ctx_len
    tm = min(256, ctx_len)

    c_rows = jnp.concatenate([c, c_ctx[None, :], jnp.zeros((8 - batch - 1, d), F32)], axis=0)
    mods_all = _modulation(c_rows, mod_w, mod_b)

    xl = x.reshape(n_lat, d)
    xc = ctx.reshape(n_ctx, d)

    for i in range(depth):
        kind, j = i % 4, i // 4
        m_lat = _split_mods(mods_all[i, :batch], d)
        m_ctx = _split_mods(mods_all[i, batch:batch + 1], d)
        g = norm_g[i]
        g0 = g[0:1]
        w1 = mlp_w1[i].astype(BF16)
        w2 = mlp_w2[i].astype(BF16)
        ctx_later = any((l % 4) in (0, 1) for l in range(i + 1, depth))
        zc = None
        if kind == 0:
            inner = ssd_w_out.shape[1]
            nh = ssd_a_log.shape[2]
            w_in = ssd_w_in[j].astype(BF16)
            n_conv = ssd_conv_w.shape[2]
            pad_rows = 128 - 2 * nh
            w_dtt = jnp.pad(w_in[:, inner + n_conv:].T, ((0, pad_rows), (0, 0)))
            dt_bias_col = jnp.pad(ssd_dt_bias[j].reshape(2 * nh, 1), ((0, pad_rows), (0, 0)))
            prm = (w_in, w_dtt, ssd_conv_w[j], ssd_conv_b[j][None, :], dt_bias_col, inner, 2 * nh)
            a_row = ssd_a_log[j].reshape(1, 2 * nh)
            a_col = ssd_a_log[j].reshape(2 * nh, 1)
            d_vec = jnp.repeat(ssd_d[j].sum(0), SSD_HEADDIM)[None, :]
            ng = ssd_norm_g[j][None, :]
            w_out = ssd_w_out[j].astype(BF16)
            tb = min(256, ctx_len)

            def mixer(rows2, seq_rows, group_rows, shift, scale, state0):
                z, xs, bcf, bcb, dt, dtt = _ssd_inproj(rows2, seq_rows, g0, shift, scale, group_rows, prm, tm)
                yf, yb, s_fin = _ssd_scan(xs, bcf, bcb, dt, dtt, a_row, a_col, state0, batch, tb)
                return (yf, yb, xs, z, d_vec, ng), s_fin

            zero_state = jnp.zeros((batch, 2, SSD_STATE, inner), F32)
            zc, ctx_state = mixer(xc, ctx_len, n_ctx, m_ctx[0], m_ctx[1], zero_state)
            zl, _ = mixer(xl, seq, seq, m_lat[0], m_lat[1], ctx_state)
        elif kind == 1:
            w_qkv = na_w_qkv[j].astype(BF16)
            w_out = na_w_out[j].astype(BF16)
            qc, kc, vc = _qkv(xc, g0, m_ctx[0], m_ctx[1], n_ctx, w_qkv, min(2 * FINISH_SUB, n_ctx))
            q, k, v = _qkv(xl, g0, m_lat[0], m_lat[1], seq, w_qkv, min(2 * FINISH_SUB, seq))
            kc = kc.reshape(batch, ctx_len, d)
            vc = vc.reshape(batch, ctx_len, d)
            zl = _na_attention(q, k, v, kc, vc, _na_bias_table(na_rpb[j]), batch)
            if ctx_later:
                raise NotImplementedError("context self-attention is only needed when a later layer reads ctx")
        elif kind == 2:
            w_out = sc_w_out[j].astype(BF16)
            zl = _sconv(xl, seq, g0, m_lat[0], m_lat[1], sc_w_in[j].astype(BF16), sc_conv_w[j], tm)
            if ctx_later:
                zc = _sconv(xc, ctx_len, g0, m_ctx[0], m_ctx[1], sc_w_in[j].astype(BF16), sc_conv_w[j], tm)
        else:
            w_out = fn_w_out[j].astype(BF16)
            zl = _fourier(xl.reshape(batch, seq, d), g0, m_lat[0], m_lat[1])
            if ctx_later:
                raise NotImplementedError("context Fourier mixing is only needed when a later layer reads ctx")
        tm_fin = min((2 if kind == 0 else 4) * FINISH_SUB, seq)
        xl = _finish(xl, zl, w_out, m_lat, g, w1, w2, seq, tm_fin)
        if ctx_later:
            xc = _finish(xc, zc, w_out, m_ctx, g, w1, w2, n_ctx, min(tm_fin, n_ctx))
    return xl.reshape(batch, seq, d)
```

```python
import functools
import math

import jax
import jax.numpy as jnp
import numpy as np
from jax import lax
from jax.experimental import pallas as pl
from jax.experimental.pallas import tpu as pltpu

F32 = jnp.float32
BF16 = jnp.bfloat16
HIGHEST = lax.Precision.HIGHEST

RMS_EPS = 1e-6
N_MOD = 6
GRID_W = 64
HALO = 8
SSD_CHUNK = 128
SSD_HEADDIM = 64
SSD_STATE = 128
SSD_GROUPS = 4
NA_HEADS = 16
NA_HEADDIM = 64
NA_WIN_ROWS = 8
NA_WIN_COLS = 16
NA_STAGE_PAIRS = 4
FN_GROUPS = 8
DFT_ROWS = 64
FN_SUB = 8
NEG_BIG = -1e30
LOG2_E = math.log2(math.e)
VMEM_LIMIT = 56 * 1024 * 1024
FINISH_SUB = 256


def _cparams(*sem):
    return pltpu.CompilerParams(dimension_semantics=sem, vmem_limit_bytes=VMEM_LIMIT)


def _full_spec(shape):
    nd = len(shape)
    return pl.BlockSpec(shape, lambda *_: (0,) * nd)


def _rms(t, g):
    return t * lax.rsqrt(jnp.mean(t * t, axis=-1, keepdims=True) + RMS_EPS) * g


def _silu(t):
    half = 0.5 * t
    return half + half * jnp.tanh(half)


def _softplus(t):
    return jnp.maximum(t, 0.0) + jnp.log1p(jnp.exp(-jnp.abs(t)))


def _dot(a, b, **kw):
    return jnp.dot(a, b, preferred_element_type=F32, **kw)


def _dot_nt(a, b, **kw):
    return lax.dot_general(a, b, (((1,), (1,)), ((), ())), preferred_element_type=F32, **kw)


def _mod_kernel(c_ref, w_ref, b_ref, o_ref):
    act = _silu(c_ref[...]).astype(BF16)
    o_ref[0] = _dot(act, w_ref[0].astype(BF16)) + b_ref[0]


def _modulation(c_rows, mod_w, mod_b):
    depth, d, n = mod_w.shape
    tn = n // 4
    rows = c_rows.shape[0]
    return pl.pallas_call(
        _mod_kernel,
        grid=(depth, n // tn),
        in_specs=[
            pl.BlockSpec((rows, d), lambda i, j: (0, 0)),
            pl.BlockSpec((1, d, tn), lambda i, j: (i, 0, j)),
            pl.BlockSpec((1, 1, tn), lambda i, j: (i, 0, j)),
        ],
        out_specs=pl.BlockSpec((1, rows, tn), lambda i, j: (i, 0, j)),
        out_shape=jax.ShapeDtypeStruct((depth, rows, n), F32),
        compiler_params=_cparams("parallel", "parallel"),
        name="modulation",
    )(c_rows, mod_w, mod_b.reshape(depth, 1, n))


def _row_specs(tm, d, tiles_per_group):
    x_spec = pl.BlockSpec((tm, d), lambda i: (i, 0))
    m_spec = pl.BlockSpec((1, 1, d), lambda i: (i // tiles_per_group, 0, 0))
    return x_spec, m_spec


def _halo_specs(tm, d, n_rows):
    per = tm // HALO
    last = n_rows // HALO - 1
    prev = pl.BlockSpec((HALO, d), lambda i: (jnp.maximum(i * per - 1, 0), 0))
    nxt = pl.BlockSpec((HALO, d), lambda i: (jnp.minimum((i + 1) * per, last), 0))
    return prev, nxt


def _normed_with_halo(xp_ref, x_ref, xn_ref, g_ref, sh_ref, sc_ref, tiles_per_seq):
    i = pl.program_id(0) % tiles_per_seq
    norm = lambda ref: _rms(ref[...], g_ref[...]) * (1.0 + sc_ref[0]) + sh_ref[0]
    keep_prev = jnp.where(i == 0, 0.0, 1.0)
    keep_next = jnp.where(i == tiles_per_seq - 1, 0.0, 1.0)
    return jnp.concatenate([norm(xp_ref) * keep_prev, norm(x_ref), norm(xn_ref) * keep_next], axis=0)


def _conv3(v, w_ref, tm):
    n, c = v.shape
    v3 = v.reshape(n // HALO, HALO, c)
    sub = lax.broadcasted_iota(jnp.int32, (1, HALO, 1), 1)
    down = pltpu.roll(v3, 1, 1)
    up = pltpu.roll(v3, HALO - 1, 1)
    k0, k1 = 1, 1 + tm // HALO
    prev = jnp.where(sub == 0, down[k0 - 1:k1 - 1], down[k0:k1])
    nxt = jnp.where(sub == HALO - 1, up[k0 + 1:k1 + 1], up[k0:k1])
    out = prev * w_ref[0:1, :][None] + v3[k0:k1] * w_ref[1:2, :][None] + nxt * w_ref[2:3, :][None]
    return out.reshape(tm, c)


SSD_CONV_TILE = 1024


def _ssd_inproj_kernel(xp_ref, x_ref, xn_ref, g_ref, sh_ref, sc_ref, w_ref, wdtt_ref, cw_ref, cb_ref, dtbt_ref,
                       z_ref, xs_ref, bcf_ref, bcb_ref, dt_ref, dtt_ref, *, tm, tiles_per_seq, inner):
    h_all = _normed_with_halo(xp_ref, x_ref, xn_ref, g_ref, sh_ref, sc_ref, tiles_per_seq)
    hb_all = h_all.astype(BF16)
    hb = h_all[HALO:HALO + tm].astype(BF16)

    z_ref[...] = _dot(hb, w_ref[:, 0:inner]).astype(BF16)
    nh2 = dtt_ref.shape[0]
    dtt = _softplus(_dot_nt(wdtt_ref[...], hb) + dtbt_ref[...])
    dtt_ref[...] = dtt[:nh2]
    dt_ref[...] = dtt.T[:, :nh2]

    cw = SSD_CONV_TILE
    for n0 in range(0, cw_ref.shape[1], cw):
        raw = _dot(hb_all, w_ref[:, inner + n0:inner + n0 + cw])
        act = _silu(_conv3(raw, cw_ref.at[:, n0:n0 + cw], tm) + cb_ref[:, n0:n0 + cw]).astype(BF16)
        if n0 < inner:
            xs_ref[:, n0:n0 + cw] = act
        elif n0 < inner + cw:
            bcf_ref[...] = act
        else:
            bcb_ref[...] = act


def _ssd_inproj(x2, n_seq_rows, g0, shift, scale, tiles_per_group_rows, prm, tm):
    n_rows, d = x2.shape
    w_in, wdtt, conv_w, conv_b, dtbt, inner, nh2 = prm
    nbc = (conv_w.shape[1] - inner) // 2
    assert nbc == SSD_CONV_TILE and inner % SSD_CONV_TILE == 0
    x_spec, m_spec = _row_specs(tm, d, tiles_per_group_rows // tm)
    prev, nxt = _halo_specs(tm, d, n_rows)
    kern = functools.partial(_ssd_inproj_kernel, tm=tm, tiles_per_seq=n_seq_rows // tm, inner=inner)
    row_spec = lambda n: pl.BlockSpec((tm, n), lambda i: (i, 0))
    return pl.pallas_call(
        kern,
        grid=(n_rows // tm,),
        in_specs=[prev, x_spec, nxt, _full_spec((1, d)), m_spec, m_spec, _full_spec(w_in.shape),
                  _full_spec(wdtt.shape), _full_spec(conv_w.shape), _full_spec(conv_b.shape), _full_spec(dtbt.shape)],
        out_specs=[row_spec(inner), row_spec(inner), row_spec(nbc), row_spec(nbc), row_spec(nh2),
                   pl.BlockSpec((nh2, tm), lambda i: (0, i))],
        out_shape=[jax.ShapeDtypeStruct((n_rows, inner), BF16),
                   jax.ShapeDtypeStruct((n_rows, inner), BF16),
                   jax.ShapeDtypeStruct((n_rows, nbc), BF16),
                   jax.ShapeDtypeStruct((n_rows, nbc), BF16),
                   jax.ShapeDtypeStruct((n_rows, nh2), F32),
                   jax.ShapeDtypeStruct((nh2, n_rows), F32)],
        compiler_params=_cparams("parallel"),
        name="ssd_inproj",
    )(x2, x2, x2, g0, shift, scale, w_in, wdtt, conv_w, conv_b, dtbt)


def _ssd_chunk(d, r0, x_ref, bc_ref, dt_ref, dtt_ref, a_row, a_col, s_ref, y_ref, tri, mask, lane_lo):
    nh = a_row.shape[1] // 2
    hpg = nh // SSD_GROUPS
    rows = pl.ds(r0, SSD_CHUNK)
    dt = dt_ref[rows, :]
    dtt = dtt_ref[:, rows]
    cum = _dot(tri, dt * a_row, precision=HIGHEST) * LOG2_E
    cumt = _dot_nt(dtt * a_col, tri, precision=HIGHEST) * LOG2_E
    end = SSD_CHUNK - 1 if d == 0 else 0
    tot = cum[end:end + 1, :]
    tott = cumt[:, end:end + 1]
    wt = jnp.exp2(tott - cumt) * dtt
    dec_s = jnp.exp2(tot)
    cumt_dt = cumt - jnp.log2(dtt)
    for g in range(SSD_GROUPS):
        bg = bc_ref[rows, g * SSD_STATE:(g + 1) * SSD_STATE]
        cg = bc_ref[rows, (SSD_GROUPS + g) * SSD_STATE:(SSD_GROUPS + g + 1) * SSD_STATE]
        cb = _dot_nt(cg, bg)
        bt = bg.astype(F32).T
        cgf = cg.astype(F32)
        for p in range(hpg // 2):
            l0 = (g * hpg + 2 * p) * SSD_HEADDIM
            x2b = x_ref[rows, l0:l0 + 2 * SSD_HEADDIM]
            s2 = s_ref[0, d, :, l0:l0 + 2 * SSD_HEADDIM]
            rhs = jnp.concatenate([x2b, s2.astype(BF16)], axis=0)
            lhs_out, lhs_up, decs = [], [], []
            for hx in range(2):
                col = d * nh + g * hpg + 2 * p + hx
                colb = jnp.broadcast_to(cum[:, col:col + 1], (SSD_CHUNK, SSD_CHUNK))
                rowb = cumt_dt[col:col + 1, :]
                sc = (cb * jnp.exp2(jnp.where(mask, colb - rowb, NEG_BIG))).astype(BF16)
                co = (cgf * jnp.exp2(colb)).astype(BF16)
                lhs_out.append(jnp.concatenate([sc, co], axis=1))
                lhs_up.append((bt * wt[col:col + 1, :]).astype(BF16))
                decs.append(dec_s[:, col:col + 1])
            out2 = _dot(jnp.concatenate(lhs_out, axis=0), rhs)
            up2 = _dot(jnp.concatenate(lhs_up, axis=0), x2b)
            y_ref[rows, l0:l0 + 2 * SSD_HEADDIM] = jnp.where(
                lane_lo, out2[:SSD_CHUNK], out2[SSD_CHUNK:]).astype(BF16)
            s_ref[0, d, :, l0:l0 + 2 * SSD_HEADDIM] = (
                s2 * jnp.where(lane_lo, decs[0], decs[1])
                + jnp.where(lane_lo, up2[:SSD_STATE], up2[SSD_STATE:]))


def _ssd_scan_kernel(xf_ref, xb_ref, bcf_ref, bcb_ref, dtf_ref, dtb_ref, dttf_ref, dttb_ref,
                     al_ref, alt_ref, s0_ref, yf_ref, yb_ref, s_ref, *, n_chunk):
    @pl.when(pl.program_id(1) == 0)
    def _():
        s_ref[...] = s0_ref[...]

    a_row = -jnp.exp(al_ref[...])
    a_col = -jnp.exp(alt_ref[...])
    ii = lax.broadcasted_iota(jnp.int32, (SSD_CHUNK, SSD_CHUNK), 0)
    jj = lax.broadcasted_iota(jnp.int32, (SSD_CHUNK, SSD_CHUNK), 1)
    lane_lo = lax.broadcasted_iota(jnp.int32, (1, 2 * SSD_HEADDIM), 1) < SSD_HEADDIM
    for c in range(n_chunk):
        low = jj <= ii
        _ssd_chunk(0, c * SSD_CHUNK, xf_ref, bcf_ref, dtf_ref, dttf_ref, a_row, a_col, s_ref, yf_ref,
                   jnp.where(low, 1.0, 0.0), low, lane_lo)
        up = jj >= ii
        _ssd_chunk(1, (n_chunk - 1 - c) * SSD_CHUNK, xb_ref, bcb_ref, dtb_ref, dttb_ref, a_row, a_col, s_ref,
                   yb_ref, jnp.where(up, 1.0, 0.0), up, lane_lo)


def _ssd_scan(xs, bcf, bcb, dt, dtt, a_log_row, a_log_col, state0, batch, tb):
    n_rows, inner = xs.shape
    nbc = bcf.shape[1]
    nh2 = dt.shape[1]
    nb = n_rows // batch // tb
    fwd = lambda b, k: (b * nb + k, 0)
    bwd = lambda b, k: (b * nb + nb - 1 - k, 0)
    fwd_t = lambda b, k: (0, b * nb + k)
    bwd_t = lambda b, k: (0, b * nb + nb - 1 - k)
    s_spec = pl.BlockSpec((1, 2, SSD_STATE, inner), lambda b, k: (b, 0, 0, 0))
    kern = functools.partial(_ssd_scan_kernel, n_chunk=tb // SSD_CHUNK)
    return pl.pallas_call(
        kern,
        grid=(batch, nb),
        in_specs=[pl.BlockSpec((tb, inner), fwd), pl.BlockSpec((tb, inner), bwd),
                  pl.BlockSpec((tb, nbc), fwd), pl.BlockSpec((tb, nbc), bwd),
                  pl.BlockSpec((tb, nh2), fwd), pl.BlockSpec((tb, nh2), bwd),
                  pl.BlockSpec((nh2, tb), fwd_t), pl.BlockSpec((nh2, tb), bwd_t),
                  _full_spec(a_log_row.shape), _full_spec(a_log_col.shape), s_spec],
        out_specs=[pl.BlockSpec((tb, inner), fwd), pl.BlockSpec((tb, inner), bwd), s_spec],
        out_shape=[jax.ShapeDtypeStruct((n_rows, inner), BF16),
                   jax.ShapeDtypeStruct((n_rows, inner), BF16),
                   jax.ShapeDtypeStruct(state0.shape, F32)],
        compiler_params=_cparams("parallel", "arbitrary"),
        name="ssd_scan",
    )(xs, xs, bcf, bcb, dt, dt, dtt, dtt, a_log_row, a_log_col, state0)


def _ssd_gate(yf_ref, yb_ref, xs_ref, z_ref, d_ref, g_ref):
    y = yf_ref[...].astype(F32) + yb_ref[...].astype(F32) + d_ref[...] * xs_ref[...].astype(F32)
    y = y * _silu(z_ref[...].astype(F32))
    return _rms(y, g_ref[...]).astype(BF16)


def _qkv_kernel(x_ref, g_ref, sh_ref, sc_ref, w_ref, q_ref, k_ref, v_ref):
    d = x_ref.shape[1]
    for r0 in range(0, x_ref.shape[0], FINISH_SUB):
        rows = pl.ds(r0, FINISH_SUB)
        hb = (_rms(x_ref[rows, :], g_ref[...]) * (1.0 + sc_ref[0]) + sh_ref[0]).astype(BF16)
        q_ref[rows, :] = (_dot(hb, w_ref[:, 0:d]) * (NA_HEADDIM ** -0.5)).astype(BF16)
        k_ref[rows, :] = _dot(hb, w_ref[:, d:2 * d]).astype(BF16)
        v_ref[rows, :] = _dot(hb, w_ref[:, 2 * d:3 * d]).astype(BF16)


def _qkv(x2, g0, shift, scale, group_rows, w, tm):
    n_rows, d = x2.shape
    x_spec, m_spec = _row_specs(tm, d, group_rows // tm)
    out = jax.ShapeDtypeStruct((n_rows, d), BF16)
    return pl.pallas_call(
        _qkv_kernel,
        grid=(n_rows // tm,),
        in_specs=[x_spec, _full_spec((1, d)), m_spec, m_spec, _full_spec(w.shape)],
        out_specs=[x_spec, x_spec, x_spec],
        out_shape=[out, out, out],
        compiler_params=_cparams("parallel"),
        name="na_qkv",
    )(x2, g0, shift, scale, w)


def _na_kernel(q_ref, k_ref, v_ref, kc_ref, vc_ref, bias_ref, o_ref):
    r = pl.program_id(1)
    rows = pl.num_programs(1)
    ro0 = jnp.clip(r - NA_WIN_ROWS // 2, 0, rows - NA_WIN_ROWS) - r + NA_WIN_ROWS - 1
    lane_lo = lax.broadcasted_iota(jnp.int32, (1, 2 * NA_HEADDIM), 1) < NA_HEADDIM
    pair_lanes = [slice(p * 2 * NA_HEADDIM, (p + 1) * 2 * NA_HEADDIM) for p in range(NA_HEADS // 2)]
    nq = q_ref.shape[0]

    def qk(p):
        lanes = pair_lanes[p]
        q2 = q_ref[:, lanes]
        zero = jnp.zeros_like(q2)
        qs = jnp.concatenate([jnp.where(lane_lo, q2, zero), jnp.where(lane_lo, zero, q2)], axis=0)
        return _dot_nt(qs, k_ref[:, lanes]), _dot_nt(qs, kc_ref[0, :, lanes])

    def softmax(p, s_loc, s_ctx):
        bias = jnp.concatenate(
            [jnp.concatenate([bias_ref[2 * p + hx, ro0 + 2 * w2] for w2 in range(NA_WIN_ROWS // 2)], axis=1)
             for hx in range(2)], axis=0)
        s_loc = s_loc + bias
        m = jnp.maximum(jnp.max(s_loc, axis=-1, keepdims=True), jnp.max(s_ctx, axis=-1, keepdims=True))
        p_loc = jnp.exp(s_loc - m)
        p_ctx = jnp.exp(s_ctx - m)
        den = jnp.sum(p_loc, axis=-1, keepdims=True) + jnp.sum(p_ctx, axis=-1, keepdims=True)
        return p_loc.astype(BF16), p_ctx.astype(BF16), den

    def pv(p, p_loc, p_ctx, den):
        lanes = pair_lanes[p]
        o = (_dot(p_loc, v_ref[:, lanes]) + _dot(p_ctx, vc_ref[0, :, lanes])) / den
        o_ref[:, lanes] = jnp.where(lane_lo, o[:nq], o[nq:]).astype(BF16)

    n_pairs = len(pair_lanes)
    for p0 in range(0, n_pairs, NA_STAGE_PAIRS):
        group = range(p0, p0 + NA_STAGE_PAIRS)
        scores = [qk(p) for p in group]
        probs = [softmax(p, *s) for p, s in zip(group, scores)]
        for p, pr in zip(group, probs):
            pv(p, *pr)


def _bias_expand_kernel(rpb_ref, sel_ref, o_ref):
    o_ref[...] = _dot(rpb_ref[...], sel_ref[...], precision=HIGHEST)


def _na_bias_table(rpb):
    n_heads, n_ro, n_co = rpb.shape
    cols = np.arange(GRID_W)
    c0 = np.clip(cols - NA_WIN_COLS // 2, 0, GRID_W - NA_WIN_COLS)
    valid = (cols[None, :] >= c0[:, None]) & (cols[None, :] < c0[:, None] + NA_WIN_COLS)
    co = cols[None, :] - cols[:, None] + NA_WIN_COLS - 1
    sel = np.zeros((n_co + 1, GRID_W, GRID_W), np.float32)
    qi, ki = np.nonzero(valid)
    sel[co[qi, ki], qi, ki] = 1.0
    sel[n_co][~valid] = NEG_BIG
    sel = jnp.asarray(sel.reshape(n_co + 1, GRID_W * GRID_W))
    rows = jnp.concatenate([rpb.reshape(n_heads * n_ro, n_co), jnp.ones((n_heads * n_ro, 1), F32)], axis=1)
    toe = pl.pallas_call(
        _bias_expand_kernel,
        out_shape=jax.ShapeDtypeStruct((n_heads * n_ro, GRID_W * GRID_W), F32),
        name="na_bias_expand",
    )(rows, sel).reshape(n_heads, n_ro, GRID_W, GRID_W)
    return jnp.concatenate([toe[:, :-1], toe[:, 1:]], axis=-1)


def _na_attention(q, k, v, kc, vc, bias_tab, batch):
    n_rows, d = q.shape
    rows = n_rows // batch // GRID_W
    wr = NA_WIN_ROWS
    start = lambda r: jnp.clip(r - wr // 2, 0, rows - wr)
    q_spec = pl.BlockSpec((GRID_W, d), lambda b, r: (b * rows + r, 0))
    win = pl.BlockSpec((pl.Element(wr * GRID_W), pl.Element(d)),
                       lambda b, r: ((b * rows + start(r)) * GRID_W, 0))
    c_spec = pl.BlockSpec((1,) + kc.shape[1:], lambda b, r: (b, 0, 0))
    return pl.pallas_call(
        _na_kernel,
        grid=(batch, rows),
        in_specs=[q_spec, win, win, c_spec, c_spec, _full_spec(bias_tab.shape)],
        out_specs=q_spec,
        out_shape=jax.ShapeDtypeStruct((n_rows, d), BF16),
        compiler_params=_cparams("parallel", "arbitrary"),
        name="na_attention",
    )(q, k, v, kc, vc, bias_tab)


def _sconv_kernel(xp_ref, x_ref, xn_ref, g_ref, sh_ref, sc_ref, w_ref, cw_ref, o_ref, *, tm, tiles_per_seq):
    d = x_ref.shape[1]
    h_all = _normed_with_halo(xp_ref, x_ref, xn_ref, g_ref, sh_ref, sc_ref, tiles_per_seq)
    hb_all = h_all.astype(BF16)
    hb = h_all[HALO:HALO + tm].astype(BF16)
    cw = 512
    for n0 in range(0, d, cw):
        c_gate = _dot(hb_all, w_ref[:, d + n0:d + n0 + cw])
        u = _dot(hb_all, w_ref[:, 2 * d + n0:2 * d + n0 + cw])
        conv = _conv3(c_gate * u, cw_ref.at[:, n0:n0 + cw], tm)
        b_gate = _dot(hb, w_ref[:, n0:n0 + cw])
        o_ref[:, n0:n0 + cw] = (b_gate * conv).astype(BF16)


def _sconv(x2, seq_rows, g0, shift, scale, w, conv_w, tm):
    n_rows, d = x2.shape
    x_spec, m_spec = _row_specs(tm, d, seq_rows // tm)
    prev, nxt = _halo_specs(tm, d, n_rows)
    kern = functools.partial(_sconv_kernel, tm=tm, tiles_per_seq=seq_rows // tm)
    return pl.pallas_call(
        kern,
        grid=(n_rows // tm,),
        in_specs=[prev, x_spec, nxt, _full_spec((1, d)), m_spec, m_spec, _full_spec(w.shape),
                  _full_spec(conv_w.shape)],
        out_specs=x_spec,
        out_shape=jax.ShapeDtypeStruct((n_rows, d), BF16),
        compiler_params=_cparams("parallel"),
        name="sconv",
    )(x2, x2, x2, g0, shift, scale, w, conv_w)


def _fourier1_kernel(x_ref, g_ref, sh_ref, sc_ref, cs_ref, km_ref, twc_ref, tws_ref, yr_ref, yi_ref):
    d = g_ref.shape[1]
    gc = d // FN_GROUPS
    n = DFT_ROWS * FN_SUB
    x = x_ref[0].reshape(n, d)
    hb = (_rms(x, g_ref[...]) * (1.0 + sc_ref[0]) + sh_ref[0]).astype(BF16)
    zr, zi = [], []
    for g in range(FN_GROUPS):
        zz = _dot(hb[:, g * gc:(g + 1) * gc], cs_ref[...])
        zr.append(zz[:, :gc])
        zi.append(zz[:, gc:])
    z = jnp.concatenate([jnp.concatenate(zr, axis=1), jnp.concatenate(zi, axis=1)], axis=0)
    y = _dot(km_ref[...], z.astype(BF16))
    yr, yi = y[:n], y[n:]
    tc, ts = twc_ref[0], tws_ref[0]
    yr_ref[0] = (yr * tc + yi * ts).reshape(FN_SUB, DFT_ROWS, d)
    yi_ref[0] = (yi * tc - yr * ts).reshape(FN_SUB, DFT_ROWS, d)


def _fourier2_kernel(yr_ref, yi_ref, gk_ref, o_ref):
    t2n, _, d = yr_ref.shape[1:]
    n = t2n * FN_SUB
    y = jnp.concatenate([yr_ref[0].reshape(n, d), yi_ref[0].reshape(n, d)], axis=0)
    o_ref[0] = _dot(gk_ref[...], y.astype(BF16)).reshape(t2n, FN_SUB, d)


def _dft_tables(seq, gc):
    t2n = seq // DFT_ROWS
    eye = np.eye(FN_SUB)
    c = np.arange(gc)
    ang_c = 2.0 * np.pi * np.outer(c, c) / gc
    cs = np.concatenate([np.cos(ang_c), -np.sin(ang_c)], axis=1)
    k1 = np.arange(DFT_ROWS)
    ang1 = 2.0 * np.pi * np.outer(k1, k1) / DFT_ROWS
    c1, s1 = np.cos(ang1), np.sin(ang1)
    m = np.stack([np.stack([c1, s1], axis=1), np.stack([-s1, c1], axis=1)], axis=0)
    km = np.einsum("akbt,sr->askbtr", m, eye).reshape(2 * DFT_ROWS * FN_SUB, 2 * DFT_ROWS * FN_SUB)
    t2 = np.arange(t2n)
    ang_tw = 2.0 * np.pi * np.outer(t2, k1) / seq
    tw_shape = (t2n // FN_SUB, FN_SUB * DFT_ROWS, 1)
    ang2 = 2.0 * np.pi * np.outer(t2, t2) / t2n
    g2 = np.stack([np.cos(ang2), np.sin(ang2)], axis=1)
    gk = np.einsum("kbt,sr->ksbtr", g2, eye).reshape(t2n * FN_SUB, 2 * t2n * FN_SUB)
    return (jnp.asarray(cs, BF16), jnp.asarray(km, BF16),
            jnp.asarray(np.cos(ang_tw).reshape(tw_shape), F32), jnp.asarray(np.sin(ang_tw).reshape(tw_shape), F32),
            jnp.asarray(gk, BF16))


def _fourier(x3, g0, shift, scale):
    batch, seq, d = x3.shape
    t2n = seq // DFT_ROWS
    n1 = DFT_ROWS * FN_SUB
    cs, km, twc, tws, gk = _dft_tables(seq, d // FN_GROUPS)
    xv = x3.reshape(batch, DFT_ROWS, t2n, d)
    m_spec = pl.BlockSpec((1, 1, d), lambda b, j: (b, 0, 0))
    y1_spec = pl.BlockSpec((1, FN_SUB, DFT_ROWS, d), lambda b, j: (b, j, 0, 0))
    tw_spec = pl.BlockSpec((1, n1, 1), lambda b, j: (j, 0, 0))
    y_shape = jax.ShapeDtypeStruct((batch, t2n, DFT_ROWS, d), F32)
    yr, yi = pl.pallas_call(
        _fourier1_kernel,
        grid=(batch, t2n // FN_SUB),
        in_specs=[pl.BlockSpec((1, DFT_ROWS, FN_SUB, d), lambda b, j: (b, 0, j, 0)),
                  _full_spec((1, d)), m_spec, m_spec, _full_spec(cs.shape), _full_spec(km.shape), tw_spec, tw_spec],
        out_specs=[y1_spec, y1_spec],
        out_shape=[y_shape, y_shape],
        compiler_params=_cparams("parallel", "parallel"),
        name="fourier_stage1",
    )(xv, g0, shift, scale, cs, km, twc, tws)
    y2_spec = pl.BlockSpec((1, t2n, FN_SUB, d), lambda b, j: (b, 0, j, 0))
    out = pl.pallas_call(
        _fourier2_kernel,
        grid=(batch, DFT_ROWS // FN_SUB),
        in_specs=[y2_spec, y2_spec, _full_spec(gk.shape)],
        out_specs=y2_spec,
        out_shape=y_shape,
        compiler_params=_cparams("parallel", "parallel"),
        name="fourier_stage2",
    )(yr, yi, gk)
    return out.reshape(batch * seq, d)


def _finish_kernel(x_ref, *refs, hc, n_mixer_in):
    mixer_in = refs[:n_mixer_in]
    wo_ref, m2_ref, m3_ref, m4_ref, m5_ref, g_ref, w1_ref, w2_ref, o_ref = refs[n_mixer_in:]
    subs = [pl.ds(r0, FINISH_SUB) for r0 in range(0, x_ref.shape[0], FINISH_SUB)]

    def out_proj(rows):
        if n_mixer_in > 1:
            z = _ssd_gate(*[ref.at[rows, :] for ref in mixer_in[:4]], *mixer_in[4:])
        else:
            z = mixer_in[0][rows, :].astype(BF16)
        return _dot(z, wo_ref[...])

    def mlp(rows, y):
        t = x_ref[rows, :] + m2_ref[0] * _rms(y, g_ref[1:2, :])
        h2 = (_rms(t, g_ref[2:3, :]) * (1.0 + m4_ref[0]) + m3_ref[0]).astype(BF16)
        acc = None
        for n0 in range(0, w1_ref.shape[1], hc):
            a = jnp.maximum(_dot(h2, w1_ref[:, n0:n0 + hc]), 0.0)
            down = _dot((a * a).astype(BF16), w2_ref[n0:n0 + hc, :])
            acc = down if acc is None else acc + down
        o_ref[rows, :] = t + m5_ref[0] * _rms(acc, g_ref[3:4, :])

    ys = [out_proj(rows) for rows in subs]
    for rows, y in zip(subs, ys):
        mlp(rows, y)


def _finish(x2, mixer_out, w_out, mods, g, w1, w2, group_rows, tm):
    n_rows, d = x2.shape
    x_spec, m_spec = _row_specs(tm, d, group_rows // tm)
    row_spec = lambda a: pl.BlockSpec((tm, a.shape[1]), lambda i: (i, 0))
    if isinstance(mixer_out, tuple):
        mixer_specs = [row_spec(a) for a in mixer_out[:4]] + [_full_spec(a.shape) for a in mixer_out[4:]]
    else:
        mixer_out = (mixer_out,)
        mixer_specs = [row_spec(mixer_out[0])]
    kern = functools.partial(_finish_kernel, hc=1024, n_mixer_in=len(mixer_out))
    (w1_all, layer), (w2_all, _) = w1, w2
    layer_spec = lambda a: pl.BlockSpec((None,) + a.shape[1:], lambda i: (layer, 0, 0),
                                        pipeline_mode=pl.Buffered(1))
    return pl.pallas_call(
        kern,
        grid=(n_rows // tm,),
        in_specs=[x_spec] + mixer_specs + [_full_spec(w_out.shape), m_spec, m_spec, m_spec, m_spec,
                                           _full_spec(g.shape), layer_spec(w1_all), layer_spec(w2_all)],
        out_specs=x_spec,
        out_shape=jax.ShapeDtypeStruct((n_rows, d), F32),
        compiler_params=_cparams("parallel"),
        name="finish",
    )(x2, *mixer_out, w_out, mods[2], mods[3], mods[4], mods[5], g, w1_all, w2_all)


def _split_mods(m_rows, d):
    return [m_rows[:, k * d:(k + 1) * d][:, None, :] for k in range(N_MOD)]


def kernel(x, c, ctx, c_ctx, mod_w, mod_b, norm_g, mlp_w1, mlp_w2, ssd_w_in, ssd_conv_w, ssd_conv_b, ssd_dt_bias,
           ssd_a_log, ssd_d, ssd_norm_g, ssd_w_out, na_w_qkv, na_rpb, na_w_out, sc_w_in, sc_conv_w, sc_w_out,
           fn_w_out):
    batch, seq, d = x.shape
    ctx_len = ctx.shape[1]
    depth = mod_w.shape[0]
    n_lat = batch * seq
    n_ctx = batch * ctx_len
    tm = min(256, ctx_len)

    c_rows = jnp.concatenate([c, c_ctx[None, :], jnp.zeros((8 - batch - 1, d), F32)], axis=0)
    mods_all = _modulation(c_rows, mod_w, mod_b)

    xl = x.reshape(n_lat, d)
    xc = ctx.reshape(n_ctx, d)
    mlp_w1_bf = mlp_w1.astype(BF16)
    mlp_w2_bf = mlp_w2.astype(BF16)

    for i in range(depth):
        kind, j = i % 4, i // 4
        m_lat = _split_mods(mods_all[i, :batch], d)
        m_ctx = _split_mods(mods_all[i, batch:batch + 1], d)
        g = norm_g[i]
        g0 = g[0:1]
        w1, w2 = (mlp_w1_bf, i), (mlp_w2_bf, i)
        ctx_later = any((l % 4) in (0, 1) for l in range(i + 1, depth))
        zc = None
        if kind == 0:
            inner = ssd_w_out.shape[1]
            nh = ssd_a_log.shape[2]
            w_in = ssd_w_in[j].astype(BF16)
            n_conv = ssd_conv_w.shape[2]
            pad_rows = 128 - 2 * nh
            w_dtt = jnp.pad(ssd_w_in[j][:, inner + n_conv:].T.astype(BF16), ((0, pad_rows), (0, 0)))
            dt_bias_col = jnp.pad(ssd_dt_bias[j].reshape(2 * nh, 1), ((0, pad_rows), (0, 0)))
            prm = (w_in, w_dtt, ssd_conv_w[j], ssd_conv_b[j][None, :], dt_bias_col, inner, 2 * nh)
            a_row = ssd_a_log[j].reshape(1, 2 * nh)
            a_col = ssd_a_log[j].reshape(2 * nh, 1)
            d_vec = jnp.repeat(ssd_d[j].sum(0), SSD_HEADDIM)[None, :]
            ng = ssd_norm_g[j][None, :]
            w_out = ssd_w_out[j].astype(BF16)
            tb = min(256, ctx_len)

            def mixer(rows2, seq_rows, group_rows, shift, scale, state0):
                z, xs, bcf, bcb, dt, dtt = _ssd_inproj(rows2, seq_rows, g0, shift, scale, group_rows, prm, tm)
                yf, yb, s_fin = _ssd_scan(xs, bcf, bcb, dt, dtt, a_row, a_col, state0, batch, tb)
                return (yf, yb, xs, z, d_vec, ng), s_fin

            zero_state = jnp.zeros((batch, 2, SSD_STATE, inner), F32)
            zc, ctx_state = mixer(xc, ctx_len, n_ctx, m_ctx[0], m_ctx[1], zero_state)
            zl, _ = mixer(xl, seq, seq, m_lat[0], m_lat[1], ctx_state)
        elif kind == 1:
            w_qkv = na_w_qkv[j].astype(BF16)
            w_out = na_w_out[j].astype(BF16)
            qc, kc, vc = _qkv(xc, g0, m_ctx[0], m_ctx[1], n_ctx, w_qkv, min(2 * FINISH_SUB, n_ctx))
            q, k, v = _qkv(xl, g0, m_lat[0], m_lat[1], seq, w_qkv, min(2 * FINISH_SUB, seq))
            kc = kc.reshape(batch, ctx_len, d)
            vc = vc.reshape(batch, ctx_len, d)
            zl = _na_attention(q, k, v, kc, vc, _na_bias_table(na_rpb[j]), batch)
            if ctx_later:
                raise NotImplementedError("context self-attention is only needed when a later layer reads ctx")
        elif kind == 2:
            w_out = sc_w_out[j].astype(BF16)
            zl = _sconv(xl, seq, g0, m_lat[0], m_lat[1], sc_w_in[j].astype(BF16), sc_conv_w[j], min(2 * tm, seq))
            if ctx_later:
                zc = _sconv(xc, ctx_len, g0, m_ctx[0], m_ctx[1], sc_w_in[j].astype(BF16), sc_conv_w[j], tm)
        else:
            w_out = fn_w_out[j].astype(BF16)
            zl = _fourier(xl.reshape(batch, seq, d), g0, m_lat[0], m_lat[1])
            if ctx_later:
                raise NotImplementedError("context Fourier mixing is only needed when a later layer reads ctx")
        tm_fin = min((2 if kind == 0 else 4) * FINISH_SUB, seq)
        xl = _finish(xl, zl, w_out, m_lat, g, w1, w2, seq, tm_fin)
        if ctx_later:
            xc = _finish(xc, zc, w_out, m_ctx, g, w1, w2, n_ctx, min(tm_fin, n_ctx))
    return xl.reshape(batch, seq, d)
```

```python
import functools
import math

import jax
import jax.numpy as jnp
import numpy as np
from jax import lax
from jax.experimental import pallas as pl
from jax.experimental.pallas import tpu as pltpu

F32 = jnp.float32
BF16 = jnp.bfloat16
HIGHEST = lax.Precision.HIGHEST

RMS_EPS = 1e-6
N_MOD = 6
GRID_W = 64
HALO = 8
SSD_CHUNK = 128
SSD_HEADDIM = 64
SSD_STATE = 128
SSD_GROUPS = 4
NA_HEADS = 16
NA_HEADDIM = 64
NA_WIN_ROWS = 8
NA_WIN_COLS = 16
NA_STAGE_PAIRS = 4
FN_GROUPS = 8
DFT_ROWS = 64
FN_SUB = 8
NEG_BIG = -1e30
LOG2_E = math.log2(math.e)
VMEM_LIMIT = 56 * 1024 * 1024
FINISH_SUB = 256


def _cparams(*sem):
    return pltpu.CompilerParams(dimension_semantics=sem, vmem_limit_bytes=VMEM_LIMIT)


def _full_spec(shape):
    nd = len(shape)
    return pl.BlockSpec(shape, lambda *_: (0,) * nd)


def _rms(t, g):
    return t * lax.rsqrt(jnp.mean(t * t, axis=-1, keepdims=True) + RMS_EPS) * g


def _silu(t):
    half = 0.5 * t
    return half + half * jnp.tanh(half)


def _softplus(t):
    return jnp.maximum(t, 0.0) + jnp.log1p(jnp.exp(-jnp.abs(t)))


def _dot(a, b, **kw):
    return jnp.dot(a, b, preferred_element_type=F32, **kw)


def _dot_nt(a, b, **kw):
    return lax.dot_general(a, b, (((1,), (1,)), ((), ())), preferred_element_type=F32, **kw)


def _mod_kernel(c_ref, w_ref, b_ref, o_ref):
    act = _silu(c_ref[...]).astype(BF16)
    o_ref[0] = _dot(act, w_ref[0].astype(BF16)) + b_ref[0]


def _modulation(c_rows, mod_w, mod_b):
    depth, d, n = mod_w.shape
    tn = n // 4
    rows = c_rows.shape[0]
    return pl.pallas_call(
        _mod_kernel,
        grid=(depth, n // tn),
        in_specs=[
            pl.BlockSpec((rows, d), lambda i, j: (0, 0)),
            pl.BlockSpec((1, d, tn), lambda i, j: (i, 0, j)),
            pl.BlockSpec((1, 1, tn), lambda i, j: (i, 0, j)),
        ],
        out_specs=pl.BlockSpec((1, rows, tn), lambda i, j: (i, 0, j)),
        out_shape=jax.ShapeDtypeStruct((depth, rows, n), F32),
        compiler_params=_cparams("parallel", "parallel"),
        name="modulation",
    )(c_rows, mod_w, mod_b.reshape(depth, 1, n))


def _row_specs(tm, d, tiles_per_group):
    x_spec = pl.BlockSpec((tm, d), lambda i: (i, 0))
    m_spec = pl.BlockSpec((1, 1, d), lambda i: (i // tiles_per_group, 0, 0))
    return x_spec, m_spec


def _halo_specs(tm, d, n_rows):
    per = tm // HALO
    last = n_rows // HALO - 1
    prev = pl.BlockSpec((HALO, d), lambda i: (jnp.maximum(i * per - 1, 0), 0))
    nxt = pl.BlockSpec((HALO, d), lambda i: (jnp.minimum((i + 1) * per, last), 0))
    return prev, nxt


def _normed_with_halo(xp_ref, x_ref, xn_ref, g_ref, sh_ref, sc_ref, tiles_per_seq):
    i = pl.program_id(0) % tiles_per_seq
    norm = lambda ref: _rms(ref[...], g_ref[...]) * (1.0 + sc_ref[0]) + sh_ref[0]
    keep_prev = jnp.where(i == 0, 0.0, 1.0)
    keep_next = jnp.where(i == tiles_per_seq - 1, 0.0, 1.0)
    return jnp.concatenate([norm(xp_ref) * keep_prev, norm(x_ref), norm(xn_ref) * keep_next], axis=0)


def _conv3(v, w_ref, tm):
    n, c = v.shape
    v3 = v.reshape(n // HALO, HALO, c)
    sub = lax.broadcasted_iota(jnp.int32, (1, HALO, 1), 1)
    down = pltpu.roll(v3, 1, 1)
    up = pltpu.roll(v3, HALO - 1, 1)
    k0, k1 = 1, 1 + tm // HALO
    prev = jnp.where(sub == 0, down[k0 - 1:k1 - 1], down[k0:k1])
    nxt = jnp.where(sub == HALO - 1, up[k0 + 1:k1 + 1], up[k0:k1])
    out = prev * w_ref[0:1, :][None] + v3[k0:k1] * w_ref[1:2, :][None] + nxt * w_ref[2:3, :][None]
    return out.reshape(tm, c)


SSD_CONV_TILE = 1024


def _ssd_inproj_kernel(xp_ref, x_ref, xn_ref, g_ref, sh_ref, sc_ref, w_ref, wdtt_ref, cw_ref, cb_ref, dtbt_ref,
                       z_ref, xs_ref, bcf_ref, bcb_ref, dt_ref, dtt_ref, *, tm, tiles_per_seq, inner):
    h_all = _normed_with_halo(xp_ref, x_ref, xn_ref, g_ref, sh_ref, sc_ref, tiles_per_seq)
    hb_all = h_all.astype(BF16)
    hb = h_all[HALO:HALO + tm].astype(BF16)

    z_ref[...] = _dot(hb, w_ref[:, 0:inner]).astype(BF16)
    nh2 = dtt_ref.shape[0]
    dtt = _softplus(_dot_nt(wdtt_ref[...], hb) + dtbt_ref[...])
    dtt_ref[...] = dtt[:nh2]
    dt_ref[...] = dtt.T[:, :nh2]

    cw = SSD_CONV_TILE
    for n0 in range(0, cw_ref.shape[1], cw):
        raw = _dot(hb_all, w_ref[:, inner + n0:inner + n0 + cw])
        act = _silu(_conv3(raw, cw_ref.at[:, n0:n0 + cw], tm) + cb_ref[:, n0:n0 + cw]).astype(BF16)
        if n0 < inner:
            xs_ref[:, n0:n0 + cw] = act
        elif n0 < inner + cw:
            bcf_ref[...] = act
        else:
            bcb_ref[...] = act


def _ssd_inproj(x2, n_seq_rows, g0, shift, scale, tiles_per_group_rows, prm, tm):
    n_rows, d = x2.shape
    (w_in, layer), wdtt, conv_w, conv_b, dtbt, inner, nh2 = prm
    w_spec = pl.BlockSpec((None,) + w_in.shape[1:], lambda i: (layer, 0, 0), pipeline_mode=pl.Buffered(1))
    nbc = (conv_w.shape[1] - inner) // 2
    assert nbc == SSD_CONV_TILE and inner % SSD_CONV_TILE == 0
    x_spec, m_spec = _row_specs(tm, d, tiles_per_group_rows // tm)
    prev, nxt = _halo_specs(tm, d, n_rows)
    kern = functools.partial(_ssd_inproj_kernel, tm=tm, tiles_per_seq=n_seq_rows // tm, inner=inner)
    row_spec = lambda n: pl.BlockSpec((tm, n), lambda i: (i, 0))
    return pl.pallas_call(
        kern,
        grid=(n_rows // tm,),
        in_specs=[prev, x_spec, nxt, _full_spec((1, d)), m_spec, m_spec, w_spec,
                  _full_spec(wdtt.shape), _full_spec(conv_w.shape), _full_spec(conv_b.shape), _full_spec(dtbt.shape)],
        out_specs=[row_spec(inner), row_spec(inner), row_spec(nbc), row_spec(nbc), row_spec(nh2),
                   pl.BlockSpec((nh2, tm), lambda i: (0, i))],
        out_shape=[jax.ShapeDtypeStruct((n_rows, inner), BF16),
                   jax.ShapeDtypeStruct((n_rows, inner), BF16),
                   jax.ShapeDtypeStruct((n_rows, nbc), BF16),
                   jax.ShapeDtypeStruct((n_rows, nbc), BF16),
                   jax.ShapeDtypeStruct((n_rows, nh2), F32),
                   jax.ShapeDtypeStruct((nh2, n_rows), F32)],
        compiler_params=_cparams("parallel"),
        name="ssd_inproj",
    )(x2, x2, x2, g0, shift, scale, w_in, wdtt, conv_w, conv_b, dtbt)


def _ssd_chunk(d, r0, x_ref, bc_ref, dt_ref, dtt_ref, a_row, a_col, s_ref, y_ref, tri, mask, lane_lo):
    nh = a_row.shape[1] // 2
    hpg = nh // SSD_GROUPS
    rows = pl.ds(r0, SSD_CHUNK)
    dt = dt_ref[rows, :]
    dtt = dtt_ref[:, rows]
    cum = _dot(tri, dt * a_row, precision=HIGHEST) * LOG2_E
    cumt = _dot_nt(dtt * a_col, tri, precision=HIGHEST) * LOG2_E
    end = SSD_CHUNK - 1 if d == 0 else 0
    tot = cum[end:end + 1, :]
    tott = cumt[:, end:end + 1]
    wt = jnp.exp2(tott - cumt) * dtt
    dec_s = jnp.exp2(tot)
    cumt_dt = cumt - jnp.log2(dtt)
    for g in range(SSD_GROUPS):
        bg = bc_ref[rows, g * SSD_STATE:(g + 1) * SSD_STATE]
        cg = bc_ref[rows, (SSD_GROUPS + g) * SSD_STATE:(SSD_GROUPS + g + 1) * SSD_STATE]
        cb = _dot_nt(cg, bg)
        bt = bg.astype(F32).T
        cgf = cg.astype(F32)
        for p in range(hpg // 2):
            l0 = (g * hpg + 2 * p) * SSD_HEADDIM
            x2b = x_ref[rows, l0:l0 + 2 * SSD_HEADDIM]
            s2 = s_ref[0, d, :, l0:l0 + 2 * SSD_HEADDIM]
            rhs = jnp.concatenate([x2b, s2.astype(BF16)], axis=0)
            lhs_out, lhs_up, decs = [], [], []
            for hx in range(2):
                col = d * nh + g * hpg + 2 * p + hx
                colb = jnp.broadcast_to(cum[:, col:col + 1], (SSD_CHUNK, SSD_CHUNK))
                rowb = cumt_dt[col:col + 1, :]
                sc = (cb * jnp.exp2(jnp.where(mask, colb - rowb, NEG_BIG))).astype(BF16)
                co = (cgf * jnp.exp2(colb)).astype(BF16)
                lhs_out.append(jnp.concatenate([sc, co], axis=1))
                lhs_up.append((bt * wt[col:col + 1, :]).astype(BF16))
                decs.append(dec_s[:, col:col + 1])
            out2 = _dot(jnp.concatenate(lhs_out, axis=0), rhs)
            up2 = _dot(jnp.concatenate(lhs_up, axis=0), x2b)
            y_ref[rows, l0:l0 + 2 * SSD_HEADDIM] = jnp.where(
                lane_lo, out2[:SSD_CHUNK], out2[SSD_CHUNK:]).astype(BF16)
            s_ref[0, d, :, l0:l0 + 2 * SSD_HEADDIM] = (
                s2 * jnp.where(lane_lo, decs[0], decs[1])
                + jnp.where(lane_lo, up2[:SSD_STATE], up2[SSD_STATE:]))


def _ssd_scan_kernel(xf_ref, xb_ref, bcf_ref, bcb_ref, dtf_ref, dtb_ref, dttf_ref, dttb_ref,
                     al_ref, alt_ref, s0_ref, yf_ref, yb_ref, s_ref, *, n_chunk):
    @pl.when(pl.program_id(1) == 0)
    def _():
        s_ref[...] = s0_ref[...]

    a_row = -jnp.exp(al_ref[...])
    a_col = -jnp.exp(alt_ref[...])
    ii = lax.broadcasted_iota(jnp.int32, (SSD_CHUNK, SSD_CHUNK), 0)
    jj = lax.broadcasted_iota(jnp.int32, (SSD_CHUNK, SSD_CHUNK), 1)
    lane_lo = lax.broadcasted_iota(jnp.int32, (1, 2 * SSD_HEADDIM), 1) < SSD_HEADDIM
    for c in range(n_chunk):
        low = jj <= ii
        _ssd_chunk(0, c * SSD_CHUNK, xf_ref, bcf_ref, dtf_ref, dttf_ref, a_row, a_col, s_ref, yf_ref,
                   jnp.where(low, 1.0, 0.0), low, lane_lo)
        up = jj >= ii
        _ssd_chunk(1, (n_chunk - 1 - c) * SSD_CHUNK, xb_ref, bcb_ref, dtb_ref, dttb_ref, a_row, a_col, s_ref,
                   yb_ref, jnp.where(up, 1.0, 0.0), up, lane_lo)


def _ssd_scan(xs, bcf, bcb, dt, dtt, a_log_row, a_log_col, state0, batch, tb):
    n_rows, inner = xs.shape
    nbc = bcf.shape[1]
    nh2 = dt.shape[1]
    nb = n_rows // batch // tb
    fwd = lambda b, k: (b * nb + k, 0)
    bwd = lambda b, k: (b * nb + nb - 1 - k, 0)
    fwd_t = lambda b, k: (0, b * nb + k)
    bwd_t = lambda b, k: (0, b * nb + nb - 1 - k)
    s_spec = pl.BlockSpec((1, 2, SSD_STATE, inner), lambda b, k: (b, 0, 0, 0))
    kern = functools.partial(_ssd_scan_kernel, n_chunk=tb // SSD_CHUNK)
    return pl.pallas_call(
        kern,
        grid=(batch, nb),
        in_specs=[pl.BlockSpec((tb, inner), fwd), pl.BlockSpec((tb, inner), bwd),
                  pl.BlockSpec((tb, nbc), fwd), pl.BlockSpec((tb, nbc), bwd),
                  pl.BlockSpec((tb, nh2), fwd), pl.BlockSpec((tb, nh2), bwd),
                  pl.BlockSpec((nh2, tb), fwd_t), pl.BlockSpec((nh2, tb), bwd_t),
                  _full_spec(a_log_row.shape), _full_spec(a_log_col.shape), s_spec],
        out_specs=[pl.BlockSpec((tb, inner), fwd), pl.BlockSpec((tb, inner), bwd), s_spec],
        out_shape=[jax.ShapeDtypeStruct((n_rows, inner), BF16),
                   jax.ShapeDtypeStruct((n_rows, inner), BF16),
                   jax.ShapeDtypeStruct(state0.shape, F32)],
        compiler_params=_cparams("parallel", "arbitrary"),
        name="ssd_scan",
    )(xs, xs, bcf, bcb, dt, dt, dtt, dtt, a_log_row, a_log_col, state0)


def _ssd_gate(yf_ref, yb_ref, xs_ref, z_ref, d_ref, g_ref):
    y = yf_ref[...].astype(F32) + yb_ref[...].astype(F32) + d_ref[...] * xs_ref[...].astype(F32)
    y = y * _silu(z_ref[...].astype(F32))
    return _rms(y, g_ref[...]).astype(BF16)


def _qkv_kernel(x_ref, g_ref, sh_ref, sc_ref, w_ref, q_ref, k_ref, v_ref):
    d = x_ref.shape[1]
    for r0 in range(0, x_ref.shape[0], FINISH_SUB):
        rows = pl.ds(r0, FINISH_SUB)
        hb = (_rms(x_ref[rows, :], g_ref[...]) * (1.0 + sc_ref[0]) + sh_ref[0]).astype(BF16)
        q_ref[rows, :] = (_dot(hb, w_ref[:, 0:d]) * (NA_HEADDIM ** -0.5)).astype(BF16)
        k_ref[rows, :] = _dot(hb, w_ref[:, d:2 * d]).astype(BF16)
        v_ref[rows, :] = _dot(hb, w_ref[:, 2 * d:3 * d]).astype(BF16)


def _qkv(x2, g0, shift, scale, group_rows, w, tm):
    n_rows, d = x2.shape
    x_spec, m_spec = _row_specs(tm, d, group_rows // tm)
    out = jax.ShapeDtypeStruct((n_rows, d), BF16)
    return pl.pallas_call(
        _qkv_kernel,
        grid=(n_rows // tm,),
        in_specs=[x_spec, _full_spec((1, d)), m_spec, m_spec, _full_spec(w.shape)],
        out_specs=[x_spec, x_spec, x_spec],
        out_shape=[out, out, out],
        compiler_params=_cparams("parallel"),
        name="na_qkv",
    )(x2, g0, shift, scale, w)


NA_STEP_ROWS = 2
NA_SPAN_ROWS = NA_WIN_ROWS + 2


def _na_span_start(step, rows):
    return jnp.clip(step * NA_STEP_ROWS - NA_WIN_ROWS // 2, 0, rows - NA_SPAN_ROWS)


def _na_kernel(q_ref, k_ref, v_ref, kc_ref, vc_ref, bias_ref, o_ref):
    step = pl.program_id(1)
    rows = pl.num_programs(1) * NA_STEP_ROWS
    base = _na_span_start(step, rows)
    lane_lo = lax.broadcasted_iota(jnp.int32, (1, 2 * NA_HEADDIM), 1) < NA_HEADDIM
    pair_lanes = [slice(p * 2 * NA_HEADDIM, (p + 1) * 2 * NA_HEADDIM) for p in range(NA_HEADS // 2)]
    n_pairs = len(pair_lanes)
    nq = GRID_W

    def row_params(par):
        r = step * NA_STEP_ROWS + par
        r0 = jnp.clip(r - NA_WIN_ROWS // 2, 0, rows - NA_WIN_ROWS)
        win = pl.ds(pl.multiple_of((r0 - base) * GRID_W, GRID_W), NA_WIN_ROWS * GRID_W)
        return pl.ds(par * GRID_W, GRID_W), win, r0 - r + NA_WIN_ROWS - 1

    def qk(p, q_rows, win):
        lanes = pair_lanes[p]
        q2 = q_ref[q_rows, lanes]
        zero = jnp.zeros_like(q2)
        qs = jnp.concatenate([jnp.where(lane_lo, q2, zero), jnp.where(lane_lo, zero, q2)], axis=0)
        return _dot_nt(qs, k_ref[win, lanes]), _dot_nt(qs, kc_ref[0, :, lanes])

    def softmax(p, ro0, s_loc, s_ctx):
        bias = jnp.concatenate(
            [jnp.concatenate([bias_ref[2 * p + hx, ro0 + 2 * w2] for w2 in range(NA_WIN_ROWS // 2)], axis=1)
             for hx in range(2)], axis=0)
        s_loc = s_loc + bias
        m = jnp.maximum(jnp.max(s_loc, axis=-1, keepdims=True), jnp.max(s_ctx, axis=-1, keepdims=True))
        p_loc = jnp.exp(s_loc - m)
        p_ctx = jnp.exp(s_ctx - m)
        den = jnp.sum(p_loc, axis=-1, keepdims=True) + jnp.sum(p_ctx, axis=-1, keepdims=True)
        return p_loc.astype(BF16), p_ctx.astype(BF16), den

    def pv(p, q_rows, win, p_loc, p_ctx, den):
        lanes = pair_lanes[p]
        o = (_dot(p_loc, v_ref[win, lanes]) + _dot(p_ctx, vc_ref[0, :, lanes])) / den
        o_ref[q_rows, lanes] = jnp.where(lane_lo, o[:nq], o[nq:]).astype(BF16)

    params = [row_params(par) for par in range(NA_STEP_ROWS)]
    scores = [[qk(p, q_rows, win) for p in range(n_pairs)] for q_rows, win, _ in params]
    for (q_rows, win, ro0), row_scores in zip(params, scores):
        probs = [softmax(p, ro0, *s) for p, s in enumerate(row_scores)]
        for p, pr in enumerate(probs):
            pv(p, q_rows, win, *pr)


def _bias_expand_kernel(rpb_ref, sel_ref, o_ref):
    o_ref[...] = _dot(rpb_ref[...], sel_ref[...], precision=HIGHEST)


def _na_bias_table(rpb):
    n_heads, n_ro, n_co = rpb.shape
    cols = np.arange(GRID_W)
    c0 = np.clip(cols - NA_WIN_COLS // 2, 0, GRID_W - NA_WIN_COLS)
    valid = (cols[None, :] >= c0[:, None]) & (cols[None, :] < c0[:, None] + NA_WIN_COLS)
    co = cols[None, :] - cols[:, None] + NA_WIN_COLS - 1
    sel = np.zeros((n_co + 1, GRID_W, GRID_W), np.float32)
    qi, ki = np.nonzero(valid)
    sel[co[qi, ki], qi, ki] = 1.0
    sel[n_co][~valid] = NEG_BIG
    sel = jnp.asarray(sel.reshape(n_co + 1, GRID_W * GRID_W))
    rows = jnp.concatenate([rpb.reshape(n_heads * n_ro, n_co), jnp.ones((n_heads * n_ro, 1), F32)], axis=1)
    toe = pl.pallas_call(
        _bias_expand_kernel,
        out_shape=jax.ShapeDtypeStruct((n_heads * n_ro, GRID_W * GRID_W), F32),
        name="na_bias_expand",
    )(rows, sel).reshape(n_heads, n_ro, GRID_W, GRID_W)
    return jnp.concatenate([toe[:, :-1], toe[:, 1:]], axis=-1)


def _na_attention(q, k, v, kc, vc, bias_tab, batch):
    n_rows, d = q.shape
    rows = n_rows // batch // GRID_W
    steps = rows // NA_STEP_ROWS
    assert rows % NA_STEP_ROWS == 0 and rows >= NA_SPAN_ROWS
    q_spec = pl.BlockSpec((NA_STEP_ROWS * GRID_W, d), lambda b, s: (b * steps + s, 0))
    win = pl.BlockSpec((pl.Element(NA_SPAN_ROWS * GRID_W), pl.Element(d)),
                       lambda b, s: ((b * rows + _na_span_start(s, rows)) * GRID_W, 0))
    c_spec = pl.BlockSpec((1,) + kc.shape[1:], lambda b, s: (b, 0, 0))
    return pl.pallas_call(
        _na_kernel,
        grid=(batch, steps),
        in_specs=[q_spec, win, win, c_spec, c_spec, _full_spec(bias_tab.shape)],
        out_specs=q_spec,
        out_shape=jax.ShapeDtypeStruct((n_rows, d), BF16),
        compiler_params=_cparams("parallel", "arbitrary"),
        name="na_attention",
    )(q, k, v, kc, vc, bias_tab)


def _sconv_kernel(xp_ref, x_ref, xn_ref, g_ref, sh_ref, sc_ref, w_ref, cw_ref, o_ref, *, tm, tiles_per_seq):
    d = x_ref.shape[1]
    h_all = _normed_with_halo(xp_ref, x_ref, xn_ref, g_ref, sh_ref, sc_ref, tiles_per_seq)
    hb_all = h_all.astype(BF16)
    hb = h_all[HALO:HALO + tm].astype(BF16)
    cw = 512
    for n0 in range(0, d, cw):
        c_gate = _dot(hb_all, w_ref[:, d + n0:d + n0 + cw])
        u = _dot(hb_all, w_ref[:, 2 * d + n0:2 * d + n0 + cw])
        conv = _conv3(c_gate * u, cw_ref.at[:, n0:n0 + cw], tm)
        b_gate = _dot(hb, w_ref[:, n0:n0 + cw])
        o_ref[:, n0:n0 + cw] = (b_gate * conv).astype(BF16)


def _sconv(x2, seq_rows, g0, shift, scale, w, conv_w, tm):
    n_rows, d = x2.shape
    x_spec, m_spec = _row_specs(tm, d, seq_rows // tm)
    prev, nxt = _halo_specs(tm, d, n_rows)
    kern = functools.partial(_sconv_kernel, tm=tm, tiles_per_seq=seq_rows // tm)
    return pl.pallas_call(
        kern,
        grid=(n_rows // tm,),
        in_specs=[prev, x_spec, nxt, _full_spec((1, d)), m_spec, m_spec, _full_spec(w.shape),
                  _full_spec(conv_w.shape)],
        out_specs=x_spec,
        out_shape=jax.ShapeDtypeStruct((n_rows, d), BF16),
        compiler_params=_cparams("parallel"),
        name="sconv",
    )(x2, x2, x2, g0, shift, scale, w, conv_w)


def _fourier1_kernel(x_ref, g_ref, sh_ref, sc_ref, cs_ref, km_ref, twc_ref, tws_ref, yr_ref, yi_ref):
    d = g_ref.shape[1]
    gc = d // FN_GROUPS
    n = DFT_ROWS * FN_SUB
    x = x_ref[0].reshape(n, d)
    hb = (_rms(x, g_ref[...]) * (1.0 + sc_ref[0]) + sh_ref[0]).astype(BF16)
    zr, zi = [], []
    for g in range(FN_GROUPS):
        zz = _dot(hb[:, g * gc:(g + 1) * gc], cs_ref[...])
        zr.append(zz[:, :gc])
        zi.append(zz[:, gc:])
    z = jnp.concatenate([jnp.concatenate(zr, axis=1), jnp.concatenate(zi, axis=1)], axis=0)
    y = _dot(km_ref[...], z.astype(BF16))
    yr, yi = y[:n], y[n:]
    tc, ts = twc_ref[0], tws_ref[0]
    yr_ref[0] = (yr * tc + yi * ts).reshape(FN_SUB, DFT_ROWS, d)
    yi_ref[0] = (yi * tc - yr * ts).reshape(FN_SUB, DFT_ROWS, d)


def _fourier2_kernel(yr_ref, yi_ref, gk_ref, o_ref):
    t2n, _, d = yr_ref.shape[1:]
    n = t2n * FN_SUB
    y = jnp.concatenate([yr_ref[0].reshape(n, d), yi_ref[0].reshape(n, d)], axis=0)
    o_ref[0] = _dot(gk_ref[...], y.astype(BF16)).reshape(t2n, FN_SUB, d)


def _dft_tables(seq, gc):
    t2n = seq // DFT_ROWS
    eye = np.eye(FN_SUB)
    c = np.arange(gc)
    ang_c = 2.0 * np.pi * np.outer(c, c) / gc
    cs = np.concatenate([np.cos(ang_c), -np.sin(ang_c)], axis=1)
    k1 = np.arange(DFT_ROWS)
    ang1 = 2.0 * np.pi * np.outer(k1, k1) / DFT_ROWS
    c1, s1 = np.cos(ang1), np.sin(ang1)
    m = np.stack([np.stack([c1, s1], axis=1), np.stack([-s1, c1], axis=1)], axis=0)
    km = np.einsum("akbt,sr->askbtr", m, eye).reshape(2 * DFT_ROWS * FN_SUB, 2 * DFT_ROWS * FN_SUB)
    t2 = np.arange(t2n)
    ang_tw = 2.0 * np.pi * np.outer(t2, k1) / seq
    tw_shape = (t2n // FN_SUB, FN_SUB * DFT_ROWS, 1)
    ang2 = 2.0 * np.pi * np.outer(t2, t2) / t2n
    g2 = np.stack([np.cos(ang2), np.sin(ang2)], axis=1)
    gk = np.einsum("kbt,sr->ksbtr", g2, eye).reshape(t2n * FN_SUB, 2 * t2n * FN_SUB)
    return (jnp.asarray(cs, BF16), jnp.asarray(km, BF16),
            jnp.asarray(np.cos(ang_tw).reshape(tw_shape), F32), jnp.asarray(np.sin(ang_tw).reshape(tw_shape), F32),
            jnp.asarray(gk, BF16))


def _fourier(x3, g0, shift, scale):
    batch, seq, d = x3.shape
    t2n = seq // DFT_ROWS
    n1 = DFT_ROWS * FN_SUB
    cs, km, twc, tws, gk = _dft_tables(seq, d // FN_GROUPS)
    xv = x3.reshape(batch, DFT_ROWS, t2n, d)
    m_spec = pl.BlockSpec((1, 1, d), lambda b, j: (b, 0, 0))
    y1_spec = pl.BlockSpec((1, FN_SUB, DFT_ROWS, d), lambda b, j: (b, j, 0, 0))
    tw_spec = pl.BlockSpec((1, n1, 1), lambda b, j: (j, 0, 0))
    y_shape = jax.ShapeDtypeStruct((batch, t2n, DFT_ROWS, d), F32)
    yr, yi = pl.pallas_call(
        _fourier1_kernel,
        grid=(batch, t2n // FN_SUB),
        in_specs=[pl.BlockSpec((1, DFT_ROWS, FN_SUB, d), lambda b, j: (b, 0, j, 0)),
                  _full_spec((1, d)), m_spec, m_spec, _full_spec(cs.shape), _full_spec(km.shape), tw_spec, tw_spec],
        out_specs=[y1_spec, y1_spec],
        out_shape=[y_shape, y_shape],
        compiler_params=_cparams("parallel", "parallel"),
        name="fourier_stage1",
    )(xv, g0, shift, scale, cs, km, twc, tws)
    y2_spec = pl.BlockSpec((1, t2n, FN_SUB, d), lambda b, j: (b, 0, j, 0))
    out = pl.pallas_call(
        _fourier2_kernel,
        grid=(batch, DFT_ROWS // FN_SUB),
        in_specs=[y2_spec, y2_spec, _full_spec(gk.shape)],
        out_specs=y2_spec,
        out_shape=y_shape,
        compiler_params=_cparams("parallel", "parallel"),
        name="fourier_stage2",
    )(yr, yi, gk)
    return out.reshape(batch * seq, d)


def _finish_kernel(x_ref, *refs, hc, n_mixer_in):
    mixer_in = refs[:n_mixer_in]
    wo_ref, m2_ref, m3_ref, m4_ref, m5_ref, g_ref, w1_ref, w2_ref, o_ref = refs[n_mixer_in:]
    subs = [pl.ds(r0, FINISH_SUB) for r0 in range(0, x_ref.shape[0], FINISH_SUB)]

    def out_proj(rows):
        if n_mixer_in > 1:
            z = _ssd_gate(*[ref.at[rows, :] for ref in mixer_in[:4]], *mixer_in[4:])
        else:
            z = mixer_in[0][rows, :].astype(BF16)
        return _dot(z, wo_ref[...])

    def mlp(rows, y):
        t = x_ref[rows, :] + m2_ref[0] * _rms(y, g_ref[1:2, :])
        h2 = (_rms(t, g_ref[2:3, :]) * (1.0 + m4_ref[0]) + m3_ref[0]).astype(BF16)
        acc = None
        for n0 in range(0, w1_ref.shape[1], hc):
            a = jnp.maximum(_dot(h2, w1_ref[:, n0:n0 + hc]), 0.0)
            down = _dot((a * a).astype(BF16), w2_ref[n0:n0 + hc, :])
            acc = down if acc is None else acc + down
        o_ref[rows, :] = t + m5_ref[0] * _rms(acc, g_ref[3:4, :])

    ys = [out_proj(rows) for rows in subs]
    for rows, y in zip(subs, ys):
        mlp(rows, y)


def _finish(x2, mixer_out, w_out, mods, g, w1, w2, group_rows, tm):
    n_rows, d = x2.shape
    x_spec, m_spec = _row_specs(tm, d, group_rows // tm)
    row_spec = lambda a: pl.BlockSpec((tm, a.shape[1]), lambda i: (i, 0))
    if isinstance(mixer_out, tuple):
        mixer_specs = [row_spec(a) for a in mixer_out[:4]] + [_full_spec(a.shape) for a in mixer_out[4:]]
    else:
        mixer_out = (mixer_out,)
        mixer_specs = [row_spec(mixer_out[0])]
    kern = functools.partial(_finish_kernel, hc=1024, n_mixer_in=len(mixer_out))
    (w1_all, layer), (w2_all, _) = w1, w2
    layer_spec = lambda a: pl.BlockSpec((None,) + a.shape[1:], lambda i: (layer, 0, 0),
                                        pipeline_mode=pl.Buffered(1))
    return pl.pallas_call(
        kern,
        grid=(n_rows // tm,),
        in_specs=[x_spec] + mixer_specs + [_full_spec(w_out.shape), m_spec, m_spec, m_spec, m_spec,
                                           _full_spec(g.shape), layer_spec(w1_all), layer_spec(w2_all)],
        out_specs=x_spec,
        out_shape=jax.ShapeDtypeStruct((n_rows, d), F32),
        compiler_params=_cparams("parallel"),
        name="finish",
    )(x2, *mixer_out, w_out, mods[2], mods[3], mods[4], mods[5], g, w1_all, w2_all)


def _split_mods(m_rows, d):
    return [m_rows[:, k * d:(k + 1) * d][:, None, :] for k in range(N_MOD)]


def kernel(x, c, ctx, c_ctx, mod_w, mod_b, norm_g, mlp_w1, mlp_w2, ssd_w_in, ssd_conv_w, ssd_conv_b, ssd_dt_bias,
           ssd_a_log, ssd_d, ssd_norm_g, ssd_w_out, na_w_qkv, na_rpb, na_w_out, sc_w_in, sc_conv_w, sc_w_out,
           fn_w_out):
    batch, seq, d = x.shape
    ctx_len = ctx.shape[1]
    depth = mod_w.shape[0]
    n_lat = batch * seq
    n_ctx = batch * ctx_len
    tm = min(256, ctx_len)

    c_rows = jnp.concatenate([c, c_ctx[None, :], jnp.zeros((8 - batch - 1, d), F32)], axis=0)
    mods_all = _modulation(c_rows, mod_w, mod_b)

    xl = x.reshape(n_lat, d)
    xc = ctx.reshape(n_ctx, d)
    mlp_w1_bf = mlp_w1.astype(BF16)
    mlp_w2_bf = mlp_w2.astype(BF16)

    for i in range(depth):
        kind, j = i % 4, i // 4
        m_lat = _split_mods(mods_all[i, :batch], d)
        m_ctx = _split_mods(mods_all[i, batch:batch + 1], d)
        g = norm_g[i]
        g0 = g[0:1]
        w1, w2 = (mlp_w1_bf, i), (mlp_w2_bf, i)
        ctx_later = any((l % 4) in (0, 1) for l in range(i + 1, depth))
        zc = None
        if kind == 0:
            inner = ssd_w_out.shape[1]
            nh = ssd_a_log.shape[2]
            w_in = (ssd_w_in.astype(BF16), j)
            n_conv = ssd_conv_w.shape[2]
            pad_rows = 128 - 2 * nh
            w_dtt = jnp.pad(ssd_w_in[j][:, inner + n_conv:].T.astype(BF16), ((0, pad_rows), (0, 0)))
            dt_bias_col = jnp.pad(ssd_dt_bias[j].reshape(2 * nh, 1), ((0, pad_rows), (0, 0)))
            prm = (w_in, w_dtt, ssd_conv_w[j], ssd_conv_b[j][None, :], dt_bias_col, inner, 2 * nh)
            a_row = ssd_a_log[j].reshape(1, 2 * nh)
            a_col = ssd_a_log[j].reshape(2 * nh, 1)
            d_vec = jnp.repeat(ssd_d[j].sum(0), SSD_HEADDIM)[None, :]
            ng = ssd_norm_g[j][None, :]
            w_out = ssd_w_out[j].astype(BF16)
            tb = min(256, ctx_len)

            def mixer(rows2, seq_rows, group_rows, shift, scale, state0):
                z, xs, bcf, bcb, dt, dtt = _ssd_inproj(rows2, seq_rows, g0, shift, scale, group_rows, prm, tm)
                yf, yb, s_fin = _ssd_scan(xs, bcf, bcb, dt, dtt, a_row, a_col, state0, batch, tb)
                return (yf, yb, xs, z, d_vec, ng), s_fin

            zero_state = jnp.zeros((batch, 2, SSD_STATE, inner), F32)
            zc, ctx_state = mixer(xc, ctx_len, n_ctx, m_ctx[0], m_ctx[1], zero_state)
            zl, _ = mixer(xl, seq, seq, m_lat[0], m_lat[1], ctx_state)
        elif kind == 1:
            w_qkv = na_w_qkv[j].astype(BF16)
            w_out = na_w_out[j].astype(BF16)
            qc, kc, vc = _qkv(xc, g0, m_ctx[0], m_ctx[1], n_ctx, w_qkv, min(2 * FINISH_SUB, n_ctx))
            q, k, v = _qkv(xl, g0, m_lat[0], m_lat[1], seq, w_qkv, min(2 * FINISH_SUB, seq))
            kc = kc.reshape(batch, ctx_len, d)
            vc = vc.reshape(batch, ctx_len, d)
            zl = _na_attention(q, k, v, kc, vc, _na_bias_table(na_rpb[j]), batch)
            if ctx_later:
                raise NotImplementedError("context self-attention is only needed when a later layer reads ctx")
        elif kind == 2:
            w_out = sc_w_out[j].astype(BF16)
            zl = _sconv(xl, seq, g0, m_lat[0], m_lat[1], sc_w_in[j].astype(BF16), sc_conv_w[j], min(2 * tm, seq))
            if ctx_later:
                zc = _sconv(xc, ctx_len, g0, m_ctx[0], m_ctx[1], sc_w_in[j].astype(BF16), sc_conv_w[j], tm)
        else:
            w_out = fn_w_out[j].astype(BF16)
            zl = _fourier(xl.reshape(batch, seq, d), g0, m_lat[0], m_lat[1])
            if ctx_later:
                raise NotImplementedError("context Fourier mixing is only needed when a later layer reads ctx")
        tm_fin = min((2 if kind == 0 else 4) * FINISH_SUB, seq)
        xl = _finish(xl, zl, w_out, m_lat, g, w1, w2, seq, tm_fin)
        if ctx_later:
            xc = _finish(xc, zc, w_out, m_ctx, g, w1, w2, n_ctx, min(tm_fin, n_ctx))
    return xl.reshape(batch, seq, d)
```

```python
import functools
import math

import jax
import jax.numpy as jnp
import numpy as np
from jax import lax
from jax.experimental import pallas as pl
from jax.experimental.pallas import tpu as pltpu

F32 = jnp.float32
BF16 = jnp.bfloat16
HIGHEST = lax.Precision.HIGHEST

RMS_EPS = 1e-6
N_MOD = 6
GRID_W = 64
HALO = 8
SSD_CHUNK = 128
SSD_HEADDIM = 64
SSD_STATE = 128
SSD_GROUPS = 4
NA_HEADS = 16
NA_HEADDIM = 64
NA_WIN_ROWS = 8
NA_WIN_COLS = 16
NA_STAGE_PAIRS = 4
FN_GROUPS = 8
DFT_ROWS = 64
FN_SUB = 8
NEG_BIG = -1e30
LOG2_E = math.log2(math.e)
VMEM_LIMIT = 56 * 1024 * 1024
FINISH_SUB = 256


def _cparams(*sem):
    return pltpu.CompilerParams(dimension_semantics=sem, vmem_limit_bytes=VMEM_LIMIT)


def _full_spec(shape):
    nd = len(shape)
    return pl.BlockSpec(shape, lambda *_: (0,) * nd)


def _rms(t, g):
    return t * lax.rsqrt(jnp.mean(t * t, axis=-1, keepdims=True) + RMS_EPS) * g


def _silu(t):
    half = 0.5 * t
    return half + half * jnp.tanh(half)


def _softplus(t):
    return jnp.maximum(t, 0.0) + jnp.log1p(jnp.exp(-jnp.abs(t)))


def _dot(a, b, **kw):
    return jnp.dot(a, b, preferred_element_type=F32, **kw)


def _dot_nt(a, b, **kw):
    return lax.dot_general(a, b, (((1,), (1,)), ((), ())), preferred_element_type=F32, **kw)


def _mod_kernel(c_ref, w_ref, b_ref, o_ref):
    act = _silu(c_ref[...]).astype(BF16)
    o_ref[0] = _dot(act, w_ref[0].astype(BF16)) + b_ref[0]


def _modulation(c_rows, mod_w, mod_b):
    depth, d, n = mod_w.shape
    tn = n // 4
    rows = c_rows.shape[0]
    return pl.pallas_call(
        _mod_kernel,
        grid=(depth, n // tn),
        in_specs=[
            pl.BlockSpec((rows, d), lambda i, j: (0, 0)),
            pl.BlockSpec((1, d, tn), lambda i, j: (i, 0, j)),
            pl.BlockSpec((1, 1, tn), lambda i, j: (i, 0, j)),
        ],
        out_specs=pl.BlockSpec((1, rows, tn), lambda i, j: (i, 0, j)),
        out_shape=jax.ShapeDtypeStruct((depth, rows, n), F32),
        compiler_params=_cparams("parallel", "parallel"),
        name="modulation",
    )(c_rows, mod_w, mod_b.reshape(depth, 1, n))


def _row_specs(tm, d, tiles_per_group):
    x_spec = pl.BlockSpec((tm, d), lambda i: (i, 0))
    m_spec = pl.BlockSpec((1, 1, d), lambda i: (i // tiles_per_group, 0, 0))
    return x_spec, m_spec


def _halo_specs(tm, d, n_rows):
    per = tm // HALO
    last = n_rows // HALO - 1
    prev = pl.BlockSpec((HALO, d), lambda i: (jnp.maximum(i * per - 1, 0), 0))
    nxt = pl.BlockSpec((HALO, d), lambda i: (jnp.minimum((i + 1) * per, last), 0))
    return prev, nxt


def _normed_with_halo(xp_ref, x_ref, xn_ref, g_ref, sh_ref, sc_ref, tiles_per_seq):
    i = pl.program_id(0) % tiles_per_seq
    norm = lambda ref: _rms(ref[...], g_ref[...]) * (1.0 + sc_ref[0]) + sh_ref[0]
    keep_prev = jnp.where(i == 0, 0.0, 1.0)
    keep_next = jnp.where(i == tiles_per_seq - 1, 0.0, 1.0)
    return jnp.concatenate([norm(xp_ref) * keep_prev, norm(x_ref), norm(xn_ref) * keep_next], axis=0)


def _conv3(v, w_ref, tm):
    n, c = v.shape
    v3 = v.reshape(n // HALO, HALO, c)
    sub = lax.broadcasted_iota(jnp.int32, (1, HALO, 1), 1)
    down = pltpu.roll(v3, 1, 1)
    up = pltpu.roll(v3, HALO - 1, 1)
    k0, k1 = 1, 1 + tm // HALO
    prev = jnp.where(sub == 0, down[k0 - 1:k1 - 1], down[k0:k1])
    nxt = jnp.where(sub == HALO - 1, up[k0 + 1:k1 + 1], up[k0:k1])
    out = prev * w_ref[0:1, :][None] + v3[k0:k1] * w_ref[1:2, :][None] + nxt * w_ref[2:3, :][None]
    return out.reshape(tm, c)


SSD_CONV_TILE = 1024


def _ssd_inproj_kernel(xp_ref, x_ref, xn_ref, g_ref, sh_ref, sc_ref, w_ref, wdtt_ref, cw_ref, cb_ref, dtbt_ref,
                       z_ref, xs_ref, bcf_ref, bcb_ref, dt_ref, dtt_ref, *, tm, tiles_per_seq, inner):
    h_all = _normed_with_halo(xp_ref, x_ref, xn_ref, g_ref, sh_ref, sc_ref, tiles_per_seq)
    hb_all = h_all.astype(BF16)
    hb = h_all[HALO:HALO + tm].astype(BF16)

    z_ref[...] = _dot(hb, w_ref[:, 0:inner]).astype(BF16)
    nh2 = dtt_ref.shape[0]
    dtt = _softplus(_dot_nt(wdtt_ref[...], hb) + dtbt_ref[...])
    dtt_ref[...] = dtt[:nh2]
    dt_ref[...] = dtt.T[:, :nh2]

    cw = SSD_CONV_TILE
    for n0 in range(0, cw_ref.shape[1], cw):
        raw = _dot(hb_all, w_ref[:, inner + n0:inner + n0 + cw])
        act = _silu(_conv3(raw, cw_ref.at[:, n0:n0 + cw], tm) + cb_ref[:, n0:n0 + cw]).astype(BF16)
        nbc = bcf_ref.shape[1]
        if n0 < inner:
            xs_ref[:, n0:n0 + cw] = act
        elif n0 < inner + nbc:
            bcf_ref[:, n0 - inner:n0 - inner + cw] = act
        else:
            bcb_ref[:, n0 - inner - nbc:n0 - inner - nbc + cw] = act


def _ssd_inproj(x2, n_seq_rows, g0, shift, scale, tiles_per_group_rows, prm, tm):
    n_rows, d = x2.shape
    (w_in, layer), wdtt, conv_w, conv_b, dtbt, inner, nh2 = prm
    w_spec = pl.BlockSpec((None,) + w_in.shape[1:], lambda i: (layer, 0, 0), pipeline_mode=pl.Buffered(1))
    nbc = (conv_w.shape[1] - inner) // 2
    assert nbc % SSD_CONV_TILE == 0 and inner % SSD_CONV_TILE == 0
    x_spec, m_spec = _row_specs(tm, d, tiles_per_group_rows // tm)
    prev, nxt = _halo_specs(tm, d, n_rows)
    kern = functools.partial(_ssd_inproj_kernel, tm=tm, tiles_per_seq=n_seq_rows // tm, inner=inner)
    row_spec = lambda n: pl.BlockSpec((tm, n), lambda i: (i, 0))
    return pl.pallas_call(
        kern,
        grid=(n_rows // tm,),
        in_specs=[prev, x_spec, nxt, _full_spec((1, d)), m_spec, m_spec, w_spec,
                  _full_spec(wdtt.shape), _full_spec(conv_w.shape), _full_spec(conv_b.shape), _full_spec(dtbt.shape)],
        out_specs=[row_spec(inner), row_spec(inner), row_spec(nbc), row_spec(nbc), row_spec(nh2),
                   pl.BlockSpec((nh2, tm), lambda i: (0, i))],
        out_shape=[jax.ShapeDtypeStruct((n_rows, inner), BF16),
                   jax.ShapeDtypeStruct((n_rows, inner), BF16),
                   jax.ShapeDtypeStruct((n_rows, nbc), BF16),
                   jax.ShapeDtypeStruct((n_rows, nbc), BF16),
                   jax.ShapeDtypeStruct((n_rows, nh2), F32),
                   jax.ShapeDtypeStruct((nh2, n_rows), F32)],
        compiler_params=_cparams("parallel"),
        name="ssd_inproj",
    )(x2, x2, x2, g0, shift, scale, w_in, wdtt, conv_w, conv_b, dtbt)


def _ssd_chunk(d, r0, x_ref, bc_ref, dt_ref, dtt_ref, a_row, a_col, s_ref, y_ref, tri, mask, lane_lo):
    nh = a_row.shape[1] // 2
    hpg = nh // SSD_GROUPS
    rows = pl.ds(r0, SSD_CHUNK)
    dt = dt_ref[rows, :]
    dtt = dtt_ref[:, rows]
    cum = _dot(tri, dt * a_row, precision=HIGHEST) * LOG2_E
    cumt = _dot_nt(dtt * a_col, tri, precision=HIGHEST) * LOG2_E
    end = SSD_CHUNK - 1 if d == 0 else 0
    tot = cum[end:end + 1, :]
    tott = cumt[:, end:end + 1]
    wt = jnp.exp2(tott - cumt) * dtt
    dec_s = jnp.exp2(tot)
    cumt_dt = cumt - jnp.log2(dtt)
    for g in range(SSD_GROUPS):
        bg = bc_ref[rows, g * SSD_STATE:(g + 1) * SSD_STATE]
        cg = bc_ref[rows, (SSD_GROUPS + g) * SSD_STATE:(SSD_GROUPS + g + 1) * SSD_STATE]
        cb = _dot_nt(cg, bg)
        bt = bg.astype(F32).T
        cgf = cg.astype(F32)
        for p in range(hpg // 2):
            l0 = (g * hpg + 2 * p) * SSD_HEADDIM
            x2b = x_ref[rows, l0:l0 + 2 * SSD_HEADDIM]
            s2 = s_ref[0, d, :, l0:l0 + 2 * SSD_HEADDIM]
            rhs = jnp.concatenate([x2b, s2.astype(BF16)], axis=0)
            lhs_out, lhs_up, decs = [], [], []
            for hx in range(2):
                col = d * nh + g * hpg + 2 * p + hx
                colb = jnp.broadcast_to(cum[:, col:col + 1], (SSD_CHUNK, SSD_CHUNK))
                rowb = cumt_dt[col:col + 1, :]
                sc = (cb * jnp.exp2(jnp.where(mask, colb - rowb, NEG_BIG))).astype(BF16)
                co = (cgf * jnp.exp2(colb)).astype(BF16)
                lhs_out.append(jnp.concatenate([sc, co], axis=1))
                lhs_up.append((bt * wt[col:col + 1, :]).astype(BF16))
                decs.append(dec_s[:, col:col + 1])
            out2 = _dot(jnp.concatenate(lhs_out, axis=0), rhs)
            up2 = _dot(jnp.concatenate(lhs_up, axis=0), x2b)
            y_ref[rows, l0:l0 + 2 * SSD_HEADDIM] = jnp.where(
                lane_lo, out2[:SSD_CHUNK], out2[SSD_CHUNK:]).astype(BF16)
            s_ref[0, d, :, l0:l0 + 2 * SSD_HEADDIM] = (
                s2 * jnp.where(lane_lo, decs[0], decs[1])
                + jnp.where(lane_lo, up2[:SSD_STATE], up2[SSD_STATE:]))


def _ssd_scan_kernel(xf_ref, xb_ref, bcf_ref, bcb_ref, dtf_ref, dtb_ref, dttf_ref, dttb_ref,
                     al_ref, alt_ref, s0_ref, yf_ref, yb_ref, s_ref, *, n_chunk):
    @pl.when(pl.program_id(1) == 0)
    def _():
        s_ref[...] = s0_ref[...]

    a_row = -jnp.exp(al_ref[...])
    a_col = -jnp.exp(alt_ref[...])
    ii = lax.broadcasted_iota(jnp.int32, (SSD_CHUNK, SSD_CHUNK), 0)
    jj = lax.broadcasted_iota(jnp.int32, (SSD_CHUNK, SSD_CHUNK), 1)
    lane_lo = lax.broadcasted_iota(jnp.int32, (1, 2 * SSD_HEADDIM), 1) < SSD_HEADDIM
    for c in range(n_chunk):
        low = jj <= ii
        _ssd_chunk(0, c * SSD_CHUNK, xf_ref, bcf_ref, dtf_ref, dttf_ref, a_row, a_col, s_ref, yf_ref,
                   jnp.where(low, 1.0, 0.0), low, lane_lo)
        up = jj >= ii
        _ssd_chunk(1, (n_chunk - 1 - c) * SSD_CHUNK, xb_ref, bcb_ref, dtb_ref, dttb_ref, a_row, a_col, s_ref,
                   yb_ref, jnp.where(up, 1.0, 0.0), up, lane_lo)


def _ssd_scan(xs, bcf, bcb, dt, dtt, a_log_row, a_log_col, state0, batch, tb):
    n_rows, inner = xs.shape
    nbc = bcf.shape[1]
    nh2 = dt.shape[1]
    nb = n_rows // batch // tb
    fwd = lambda b, k: (b * nb + k, 0)
    bwd = lambda b, k: (b * nb + nb - 1 - k, 0)
    fwd_t = lambda b, k: (0, b * nb + k)
    bwd_t = lambda b, k: (0, b * nb + nb - 1 - k)
    s_spec = pl.BlockSpec((1, 2, SSD_STATE, inner), lambda b, k: (b, 0, 0, 0))
    kern = functools.partial(_ssd_scan_kernel, n_chunk=tb // SSD_CHUNK)
    return pl.pallas_call(
        kern,
        grid=(batch, nb),
        in_specs=[pl.BlockSpec((tb, inner), fwd), pl.BlockSpec((tb, inner), bwd),
                  pl.BlockSpec((tb, nbc), fwd), pl.BlockSpec((tb, nbc), bwd),
                  pl.BlockSpec((tb, nh2), fwd), pl.BlockSpec((tb, nh2), bwd),
                  pl.BlockSpec((nh2, tb), fwd_t), pl.BlockSpec((nh2, tb), bwd_t),
                  _full_spec(a_log_row.shape), _full_spec(a_log_col.shape), s_spec],
        out_specs=[pl.BlockSpec((tb, inner), fwd), pl.BlockSpec((tb, inner), bwd), s_spec],
        out_shape=[jax.ShapeDtypeStruct((n_rows, inner), BF16),
                   jax.ShapeDtypeStruct((n_rows, inner), BF16),
                   jax.ShapeDtypeStruct(state0.shape, F32)],
        compiler_params=_cparams("parallel", "arbitrary"),
        name="ssd_scan",
    )(xs, xs, bcf, bcb, dt, dt, dtt, dtt, a_log_row, a_log_col, state0)


def _ssd_gate(yf_ref, yb_ref, xs_ref, z_ref, d_ref, g_ref):
    y = yf_ref[...].astype(F32) + yb_ref[...].astype(F32) + d_ref[...] * xs_ref[...].astype(F32)
    y = y * _silu(z_ref[...].astype(F32))
    return _rms(y, g_ref[...]).astype(BF16)


def _qkv_kernel(x_ref, g_ref, sh_ref, sc_ref, w_ref, q_ref, k_ref, v_ref):
    d = x_ref.shape[1]
    for r0 in range(0, x_ref.shape[0], FINISH_SUB):
        rows = pl.ds(r0, FINISH_SUB)
        hb = (_rms(x_ref[rows, :], g_ref[...]) * (1.0 + sc_ref[0]) + sh_ref[0]).astype(BF16)
        q_ref[rows, :] = (_dot(hb, w_ref[:, 0:d]) * (NA_HEADDIM ** -0.5 * LOG2_E)).astype(BF16)
        k_ref[rows, :] = _dot(hb, w_ref[:, d:2 * d]).astype(BF16)
        v_ref[rows, :] = _dot(hb, w_ref[:, 2 * d:3 * d]).astype(BF16)


def _qkv(x2, g0, shift, scale, group_rows, w, tm):
    n_rows, d = x2.shape
    x_spec, m_spec = _row_specs(tm, d, group_rows // tm)
    out = jax.ShapeDtypeStruct((n_rows, d), BF16)
    return pl.pallas_call(
        _qkv_kernel,
        grid=(n_rows // tm,),
        in_specs=[x_spec, _full_spec((1, d)), m_spec, m_spec, _full_spec(w.shape)],
        out_specs=[x_spec, x_spec, x_spec],
        out_shape=[out, out, out],
        compiler_params=_cparams("parallel"),
        name="na_qkv",
    )(x2, g0, shift, scale, w)


NA_STEP_ROWS = 2
NA_SPAN_ROWS = NA_WIN_ROWS + 2


def _na_span_start(step, rows):
    return jnp.clip(step * NA_STEP_ROWS - NA_WIN_ROWS // 2, 0, rows - NA_SPAN_ROWS)


def _na_kernel(q_ref, k_ref, v_ref, kc_ref, vc_ref, bias_ref, o_ref):
    step = pl.program_id(1)
    rows = pl.num_programs(1) * NA_STEP_ROWS
    base = _na_span_start(step, rows)
    lane_lo = lax.broadcasted_iota(jnp.int32, (1, 2 * NA_HEADDIM), 1) < NA_HEADDIM
    pair_lanes = [slice(p * 2 * NA_HEADDIM, (p + 1) * 2 * NA_HEADDIM) for p in range(NA_HEADS // 2)]
    n_pairs = len(pair_lanes)
    nq = GRID_W

    def row_params(par):
        r = step * NA_STEP_ROWS + par
        r0 = jnp.clip(r - NA_WIN_ROWS // 2, 0, rows - NA_WIN_ROWS)
        win = pl.ds(pl.multiple_of((r0 - base) * GRID_W, GRID_W), NA_WIN_ROWS * GRID_W)
        return pl.ds(par * GRID_W, GRID_W), win, r0 - r + NA_WIN_ROWS - 1

    def qk(p, q_rows, win):
        lanes = pair_lanes[p]
        q2 = q_ref[q_rows, lanes]
        zero = jnp.zeros_like(q2)
        qs = jnp.concatenate([jnp.where(lane_lo, q2, zero), jnp.where(lane_lo, zero, q2)], axis=0)
        return _dot_nt(qs, k_ref[win, lanes]), _dot_nt(qs, kc_ref[0, :, lanes])

    def softmax(p, ro0, s_loc, s_ctx):
        bias = jnp.concatenate(
            [jnp.concatenate([bias_ref[2 * p + hx, ro0 + 2 * w2] for w2 in range(NA_WIN_ROWS // 2)], axis=1)
             for hx in range(2)], axis=0)
        s_loc = s_loc + bias
        m = jnp.maximum(jnp.max(s_loc, axis=-1, keepdims=True), jnp.max(s_ctx, axis=-1, keepdims=True))
        p_loc = jnp.exp2(s_loc - m)
        p_ctx = jnp.exp2(s_ctx - m)
        den = jnp.sum(p_loc, axis=-1, keepdims=True) + jnp.sum(p_ctx, axis=-1, keepdims=True)
        return p_loc.astype(BF16), p_ctx.astype(BF16), den

    def pv(p, q_rows, win, p_loc, p_ctx, den):
        lanes = pair_lanes[p]
        o = (_dot(p_loc, v_ref[win, lanes]) + _dot(p_ctx, vc_ref[0, :, lanes])) / den
        o_ref[q_rows, lanes] = jnp.where(lane_lo, o[:nq], o[nq:]).astype(BF16)

    params = [row_params(par) for par in range(NA_STEP_ROWS)]
    scores = [[qk(p, q_rows, win) for p in range(n_pairs)] for q_rows, win, _ in params]
    for (q_rows, win, ro0), row_scores in zip(params, scores):
        probs = [softmax(p, ro0, *s) for p, s in enumerate(row_scores)]
        for p, pr in enumerate(probs):
            pv(p, q_rows, win, *pr)


def _bias_expand_kernel(rpb_ref, sel_ref, o_ref):
    o_ref[...] = _dot(rpb_ref[...], sel_ref[...], precision=HIGHEST)


def _na_bias_table(rpb):
    n_heads, n_ro, n_co = rpb.shape
    cols = np.arange(GRID_W)
    c0 = np.clip(cols - NA_WIN_COLS // 2, 0, GRID_W - NA_WIN_COLS)
    valid = (cols[None, :] >= c0[:, None]) & (cols[None, :] < c0[:, None] + NA_WIN_COLS)
    co = cols[None, :] - cols[:, None] + NA_WIN_COLS - 1
    sel = np.zeros((n_co + 1, GRID_W, GRID_W), np.float32)
    qi, ki = np.nonzero(valid)
    sel[co[qi, ki], qi, ki] = 1.0
    sel[n_co][~valid] = NEG_BIG
    sel = jnp.asarray(sel.reshape(n_co + 1, GRID_W * GRID_W))
    rows = jnp.concatenate([rpb.reshape(n_heads * n_ro, n_co) * LOG2_E, jnp.ones((n_heads * n_ro, 1), F32)],
                           axis=1)
    toe = pl.pallas_call(
        _bias_expand_kernel,
        out_shape=jax.ShapeDtypeStruct((n_heads * n_ro, GRID_W * GRID_W), F32),
        name="na_bias_expand",
    )(rows, sel).reshape(n_heads, n_ro, GRID_W, GRID_W)
    return jnp.concatenate([toe[:, :-1], toe[:, 1:]], axis=-1)


def _na_attention(q, k, v, kc, vc, bias_tab, batch):
    n_rows, d = q.shape
    rows = n_rows // batch // GRID_W
    steps = rows // NA_STEP_ROWS
    assert rows % NA_STEP_ROWS == 0 and rows >= NA_SPAN_ROWS
    q_spec = pl.BlockSpec((NA_STEP_ROWS * GRID_W, d), lambda b, s: (b * steps + s, 0))
    win = pl.BlockSpec((pl.Element(NA_SPAN_ROWS * GRID_W), pl.Element(d)),
                       lambda b, s: ((b * rows + _na_span_start(s, rows)) * GRID_W, 0))
    c_spec = pl.BlockSpec((1,) + kc.shape[1:], lambda b, s: (b, 0, 0))
    return pl.pallas_call(
        _na_kernel,
        grid=(batch, steps),
        in_specs=[q_spec, win, win, c_spec, c_spec, _full_spec(bias_tab.shape)],
        out_specs=q_spec,
        out_shape=jax.ShapeDtypeStruct((n_rows, d), BF16),
        compiler_params=_cparams("parallel", "arbitrary"),
        name="na_attention",
    )(q, k, v, kc, vc, bias_tab)


def _sconv_kernel(xp_ref, x_ref, xn_ref, g_ref, sh_ref, sc_ref, w_ref, cw_ref, o_ref, *, tm, tiles_per_seq):
    d = x_ref.shape[1]
    h_all = _normed_with_halo(xp_ref, x_ref, xn_ref, g_ref, sh_ref, sc_ref, tiles_per_seq)
    hb_all = h_all.astype(BF16)
    hb = h_all[HALO:HALO + tm].astype(BF16)
    cw = 512
    for n0 in range(0, d, cw):
        c_gate = _dot(hb_all, w_ref[:, d + n0:d + n0 + cw])
        u = _dot(hb_all, w_ref[:, 2 * d + n0:2 * d + n0 + cw])
        conv = _conv3(c_gate * u, cw_ref.at[:, n0:n0 + cw], tm)
        b_gate = _dot(hb, w_ref[:, n0:n0 + cw])
        o_ref[:, n0:n0 + cw] = (b_gate * conv).astype(BF16)


def _sconv(x2, seq_rows, g0, shift, scale, w, conv_w, tm):
    n_rows, d = x2.shape
    x_spec, m_spec = _row_specs(tm, d, seq_rows // tm)
    prev, nxt = _halo_specs(tm, d, n_rows)
    kern = functools.partial(_sconv_kernel, tm=tm, tiles_per_seq=seq_rows // tm)
    return pl.pallas_call(
        kern,
        grid=(n_rows // tm,),
        in_specs=[prev, x_spec, nxt, _full_spec((1, d)), m_spec, m_spec, _full_spec(w.shape),
                  _full_spec(conv_w.shape)],
        out_specs=x_spec,
        out_shape=jax.ShapeDtypeStruct((n_rows, d), BF16),
        compiler_params=_cparams("parallel"),
        name="sconv",
    )(x2, x2, x2, g0, shift, scale, w, conv_w)


def _fourier1_kernel(x_ref, g_ref, sh_ref, sc_ref, cs_ref, km_ref, twc_ref, tws_ref, yr_ref, yi_ref):
    d = g_ref.shape[1]
    gc = d // FN_GROUPS
    n = DFT_ROWS * FN_SUB
    x = x_ref[0].reshape(n, d)
    hb = (_rms(x, g_ref[...]) * (1.0 + sc_ref[0]) + sh_ref[0]).astype(BF16)
    zr, zi = [], []
    for g in range(FN_GROUPS):
        zz = _dot(hb[:, g * gc:(g + 1) * gc], cs_ref[...])
        zr.append(zz[:, :gc])
        zi.append(zz[:, gc:])
    z = jnp.concatenate([jnp.concatenate(zr, axis=1), jnp.concatenate(zi, axis=1)], axis=0)
    y = _dot(km_ref[...], z.astype(BF16))
    yr, yi = y[:n], y[n:]
    tc, ts = twc_ref[0], tws_ref[0]
    yr_ref[0] = (yr * tc + yi * ts).reshape(FN_SUB, DFT_ROWS, d)
    yi_ref[0] = (yi * tc - yr * ts).reshape(FN_SUB, DFT_ROWS, d)


def _fourier2_kernel(yr_ref, yi_ref, gk_ref, o_ref):
    t2n, _, d = yr_ref.shape[1:]
    n = t2n * FN_SUB
    y = jnp.concatenate([yr_ref[0].reshape(n, d), yi_ref[0].reshape(n, d)], axis=0)
    o_ref[0] = _dot(gk_ref[...], y.astype(BF16)).reshape(t2n, FN_SUB, d)


def _dft_tables(seq, gc):
    t2n = seq // DFT_ROWS
    eye = np.eye(FN_SUB)
    c = np.arange(gc)
    ang_c = 2.0 * np.pi * np.outer(c, c) / gc
    cs = np.concatenate([np.cos(ang_c), -np.sin(ang_c)], axis=1)
    k1 = np.arange(DFT_ROWS)
    ang1 = 2.0 * np.pi * np.outer(k1, k1) / DFT_ROWS
    c1, s1 = np.cos(ang1), np.sin(ang1)
    m = np.stack([np.stack([c1, s1], axis=1), np.stack([-s1, c1], axis=1)], axis=0)
    km = np.einsum("akbt,sr->askbtr", m, eye).reshape(2 * DFT_ROWS * FN_SUB, 2 * DFT_ROWS * FN_SUB)
    t2 = np.arange(t2n)
    ang_tw = 2.0 * np.pi * np.outer(t2, k1) / seq
    tw_shape = (t2n // FN_SUB, FN_SUB * DFT_ROWS, 1)
    ang2 = 2.0 * np.pi * np.outer(t2, t2) / t2n
    g2 = np.stack([np.cos(ang2), np.sin(ang2)], axis=1)
    gk = np.einsum("kbt,sr->ksbtr", g2, eye).reshape(t2n * FN_SUB, 2 * t2n * FN_SUB)
    return (jnp.asarray(cs, BF16), jnp.asarray(km, BF16),
            jnp.asarray(np.cos(ang_tw).reshape(tw_shape), F32), jnp.asarray(np.sin(ang_tw).reshape(tw_shape), F32),
            jnp.asarray(gk, BF16))


def _fourier(x3, g0, shift, scale):
    batch, seq, d = x3.shape
    t2n = seq // DFT_ROWS
    n1 = DFT_ROWS * FN_SUB
    cs, km, twc, tws, gk = _dft_tables(seq, d // FN_GROUPS)
    xv = x3.reshape(batch, DFT_ROWS, t2n, d)
    m_spec = pl.BlockSpec((1, 1, d), lambda b, j: (b, 0, 0))
    y1_spec = pl.BlockSpec((1, FN_SUB, DFT_ROWS, d), lambda b, j: (b, j, 0, 0))
    tw_spec = pl.BlockSpec((1, n1, 1), lambda b, j: (j, 0, 0))
    y_shape = jax.ShapeDtypeStruct((batch, t2n, DFT_ROWS, d), F32)
    yr, yi = pl.pallas_call(
        _fourier1_kernel,
        grid=(batch, t2n // FN_SUB),
        in_specs=[pl.BlockSpec((1, DFT_ROWS, FN_SUB, d), lambda b, j: (b, 0, j, 0)),
                  _full_spec((1, d)), m_spec, m_spec, _full_spec(cs.shape), _full_spec(km.shape), tw_spec, tw_spec],
        out_specs=[y1_spec, y1_spec],
        out_shape=[y_shape, y_shape],
        compiler_params=_cparams("parallel", "parallel"),
        name="fourier_stage1",
    )(xv, g0, shift, scale, cs, km, twc, tws)
    y2_spec = pl.BlockSpec((1, t2n, FN_SUB, d), lambda b, j: (b, 0, j, 0))
    out = pl.pallas_call(
        _fourier2_kernel,
        grid=(batch, DFT_ROWS // FN_SUB),
        in_specs=[y2_spec, y2_spec, _full_spec(gk.shape)],
        out_specs=y2_spec,
        out_shape=y_shape,
        compiler_params=_cparams("parallel", "parallel"),
        name="fourier_stage2",
    )(yr, yi, gk)
    return out.reshape(batch * seq, d)


def _finish_kernel(x_ref, *refs, hc, n_mixer_in):
    mixer_in = refs[:n_mixer_in]
    wo_ref, m2_ref, m3_ref, m4_ref, m5_ref, g_ref, w1_ref, w2_ref, o_ref = refs[n_mixer_in:]
    subs = [pl.ds(r0, FINISH_SUB) for r0 in range(0, x_ref.shape[0], FINISH_SUB)]

    def out_proj(rows):
        if n_mixer_in > 1:
            z = _ssd_gate(*[ref.at[rows, :] for ref in mixer_in[:4]], *mixer_in[4:])
        else:
            z = mixer_in[0][rows, :].astype(BF16)
        return _dot(z, wo_ref[...])

    def mlp(rows, y):
        t = x_ref[rows, :] + m2_ref[0] * _rms(y, g_ref[1:2, :])
        h2 = (_rms(t, g_ref[2:3, :]) * (1.0 + m4_ref[0]) + m3_ref[0]).astype(BF16)
        acc = None
        for n0 in range(0, w1_ref.shape[1], hc):
            a = jnp.maximum(_dot(h2, w1_ref[:, n0:n0 + hc]), 0.0)
            down = _dot((a * a).astype(BF16), w2_ref[n0:n0 + hc, :])
            acc = down if acc is None else acc + down
        o_ref[rows, :] = t + m5_ref[0] * _rms(acc, g_ref[3:4, :])

    ys = [out_proj(rows) for rows in subs]
    for rows, y in zip(subs, ys):
        mlp(rows, y)


def _finish(x2, mixer_out, w_out, mods, g, w1, w2, group_rows, tm):
    n_rows, d = x2.shape
    x_spec, m_spec = _row_specs(tm, d, group_rows // tm)
    row_spec = lambda a: pl.BlockSpec((tm, a.shape[1]), lambda i: (i, 0))
    if isinstance(mixer_out, tuple):
        mixer_specs = [row_spec(a) for a in mixer_out[:4]] + [_full_spec(a.shape) for a in mixer_out[4:]]
    else:
        mixer_out = (mixer_out,)
        mixer_specs = [row_spec(mixer_out[0])]
    kern = functools.partial(_finish_kernel, hc=1024, n_mixer_in=len(mixer_out))
    (w1_all, layer), (w2_all, _) = w1, w2
    layer_spec = lambda a: pl.BlockSpec((None,) + a.shape[1:], lambda i: (layer, 0, 0),
                                        pipeline_mode=pl.Buffered(1))
    return pl.pallas_call(
        kern,
        grid=(n_rows // tm,),
        in_specs=[x_spec] + mixer_specs + [_full_spec(w_out.shape), m_spec, m_spec, m_spec, m_spec,
                                           _full_spec(g.shape), layer_spec(w1_all), layer_spec(w2_all)],
        out_specs=x_spec,
        out_shape=jax.ShapeDtypeStruct((n_rows, d), F32),
        compiler_params=_cparams("parallel"),
        name="finish",
    )(x2, *mixer_out, w_out, mods[2], mods[3], mods[4], mods[5], g, w1_all, w2_all)


def _split_mods(m_rows, d):
    return [m_rows[:, k * d:(k + 1) * d][:, None, :] for k in range(N_MOD)]


def kernel(x, c, ctx, c_ctx, mod_w, mod_b, norm_g, mlp_w1, mlp_w2, ssd_w_in, ssd_conv_w, ssd_conv_b, ssd_dt_bias,
           ssd_a_log, ssd_d, ssd_norm_g, ssd_w_out, na_w_qkv, na_rpb, na_w_out, sc_w_in, sc_conv_w, sc_w_out,
           fn_w_out):
    batch, seq, d = x.shape
    ctx_len = ctx.shape[1]
    depth = mod_w.shape[0]
    n_lat = batch * seq
    n_ctx = batch * ctx_len
    tm = min(256, ctx_len)

    c_rows = jnp.concatenate([c, c_ctx[None, :], jnp.zeros((8 - batch - 1, d), F32)], axis=0)
    mods_all = _modulation(c_rows, mod_w, mod_b)

    xl = x.reshape(n_lat, d)
    xc = ctx.reshape(n_ctx, d)
    mlp_w1_bf = mlp_w1.astype(BF16)
    mlp_w2_bf = mlp_w2.astype(BF16)

    for i in range(depth):
        kind, j = i % 4, i // 4
        m_lat = _split_mods(mods_all[i, :batch], d)
        m_ctx = _split_mods(mods_all[i, batch:batch + 1], d)
        g = norm_g[i]
        g0 = g[0:1]
        w1, w2 = (mlp_w1_bf, i), (mlp_w2_bf, i)
        ctx_later = any((l % 4) in (0, 1) for l in range(i + 1, depth))
        zc = None
        if kind == 0:
            inner = ssd_w_out.shape[1]
            nh = ssd_a_log.shape[2]
            w_in = (ssd_w_in.astype(BF16), j)
            n_conv = ssd_conv_w.shape[2]
            pad_rows = 128 - 2 * nh
            w_dtt = jnp.pad(ssd_w_in[j][:, inner + n_conv:].T.astype(BF16), ((0, pad_rows), (0, 0)))
            dt_bias_col = jnp.pad(ssd_dt_bias[j].reshape(2 * nh, 1), ((0, pad_rows), (0, 0)))
            prm = (w_in, w_dtt, ssd_conv_w[j], ssd_conv_b[j][None, :], dt_bias_col, inner, 2 * nh)
            a_row = ssd_a_log[j].reshape(1, 2 * nh)
            a_col = ssd_a_log[j].reshape(2 * nh, 1)
            d_vec = jnp.repeat(ssd_d[j].sum(0), SSD_HEADDIM)[None, :]
            ng = ssd_norm_g[j][None, :]
            w_out = ssd_w_out[j].astype(BF16)
            tb = min(256, ctx_len)

            def mixer(rows2, seq_rows, group_rows, shift, scale, state0):
                z, xs, bcf, bcb, dt, dtt = _ssd_inproj(rows2, seq_rows, g0, shift, scale, group_rows, prm, tm)
                yf, yb, s_fin = _ssd_scan(xs, bcf, bcb, dt, dtt, a_row, a_col, state0, batch, tb)
                return (yf, yb, xs, z, d_vec, ng), s_fin

            zero_state = jnp.zeros((batch, 2, SSD_STATE, inner), F32)
            zc, ctx_state = mixer(xc, ctx_len, n_ctx, m_ctx[0], m_ctx[1], zero_state)
            zl, _ = mixer(xl, seq, seq, m_lat[0], m_lat[1], ctx_state)
        elif kind == 1:
            w_qkv = na_w_qkv[j].astype(BF16)
            w_out = na_w_out[j].astype(BF16)
            qc, kc, vc = _qkv(xc, g0, m_ctx[0], m_ctx[1], n_ctx, w_qkv, min(2 * FINISH_SUB, n_ctx))
            q, k, v = _qkv(xl, g0, m_lat[0], m_lat[1], seq, w_qkv, min(2 * FINISH_SUB, seq))
            kc = kc.reshape(batch, ctx_len, d)
            vc = vc.reshape(batch, ctx_len, d)
            zl = _na_attention(q, k, v, kc, vc, _na_bias_table(na_rpb[j]), batch)
            if ctx_later:
                raise NotImplementedError("context self-attention is only needed when a later layer reads ctx")
        elif kind == 2:
            w_out = sc_w_out[j].astype(BF16)
            zl = _sconv(xl, seq, g0, m_lat[0], m_lat[1], sc_w_in[j].astype(BF16), sc_conv_w[j], min(2 * tm, seq))
            if ctx_later:
                zc = _sconv(xc, ctx_len, g0, m_ctx[0], m_ctx[1], sc_w_in[j].astype(BF16), sc_conv_w[j], tm)
        else:
            w_out = fn_w_out[j].astype(BF16)
            zl = _fourier(xl.reshape(batch, seq, d), g0, m_lat[0], m_lat[1])
            if ctx_later:
                raise NotImplementedError("context Fourier mixing is only needed when a later layer reads ctx")
        tm_fin = min((2 if kind == 0 else 4) * FINISH_SUB, seq)
        xl = _finish(xl, zl, w_out, m_lat, g, w1, w2, seq, tm_fin)
        if ctx_later:
            xc = _finish(xc, zc, w_out, m_ctx, g, w1, w2, n_ctx, min(tm_fin, n_ctx))
    return xl.reshape(batch, seq, d)
```

```python
import functools
import math

import jax
import jax.numpy as jnp
import numpy as np
from jax import lax
from jax.experimental import pallas as pl
from jax.experimental.pallas import tpu as pltpu

F32 = jnp.float32
BF16 = jnp.bfloat16
HIGHEST = lax.Precision.HIGHEST

RMS_EPS = 1e-6
N_MOD = 6
GRID_W = 64
HALO = 8
SSD_CHUNK = 128
SSD_HEADDIM = 64
SSD_STATE = 128
SSD_GROUPS = 4
NA_HEADS = 16
NA_HEADDIM = 64
NA_WIN_ROWS = 8
NA_WIN_COLS = 16
NA_STAGE_PAIRS = 4
FN_GROUPS = 8
DFT_ROWS = 64
FN_SUB = 8
NEG_BIG = -1e30
LOG2_E = math.log2(math.e)
VMEM_LIMIT = 56 * 1024 * 1024
FINISH_SUB = 256


def _cparams(*sem):
    return pltpu.CompilerParams(dimension_semantics=sem, vmem_limit_bytes=VMEM_LIMIT)


def _full_spec(shape):
    nd = len(shape)
    return pl.BlockSpec(shape, lambda *_: (0,) * nd)


def _rms(t, g):
    return t * lax.rsqrt(jnp.mean(t * t, axis=-1, keepdims=True) + RMS_EPS) * g


def _silu(t):
    half = 0.5 * t
    return half + half * jnp.tanh(half)


def _softplus(t):
    return jnp.maximum(t, 0.0) + jnp.log1p(jnp.exp(-jnp.abs(t)))


def _dot(a, b, **kw):
    return jnp.dot(a, b, preferred_element_type=F32, **kw)


def _dot_nt(a, b, **kw):
    return lax.dot_general(a, b, (((1,), (1,)), ((), ())), preferred_element_type=F32, **kw)


def _mod_kernel(c_ref, w_ref, b_ref, o_ref):
    act = _silu(c_ref[...]).astype(BF16)
    o_ref[0] = _dot(act, w_ref[0].astype(BF16)) + b_ref[0]


def _modulation(c_rows, mod_w, mod_b):
    depth, d, n = mod_w.shape
    tn = n // 4
    rows = c_rows.shape[0]
    return pl.pallas_call(
        _mod_kernel,
        grid=(depth, n // tn),
        in_specs=[
            pl.BlockSpec((rows, d), lambda i, j: (0, 0)),
            pl.BlockSpec((1, d, tn), lambda i, j: (i, 0, j)),
            pl.BlockSpec((1, 1, tn), lambda i, j: (i, 0, j)),
        ],
        out_specs=pl.BlockSpec((1, rows, tn), lambda i, j: (i, 0, j)),
        out_shape=jax.ShapeDtypeStruct((depth, rows, n), F32),
        compiler_params=_cparams("parallel", "parallel"),
        name="modulation",
    )(c_rows, mod_w, mod_b.reshape(depth, 1, n))


def _row_specs(tm, d, tiles_per_group):
    x_spec = pl.BlockSpec((tm, d), lambda i: (i, 0))
    m_spec = pl.BlockSpec((1, 1, d), lambda i: (i // tiles_per_group, 0, 0))
    return x_spec, m_spec


def _halo_specs(tm, d, n_rows):
    per = tm // HALO
    last = n_rows // HALO - 1
    prev = pl.BlockSpec((HALO, d), lambda i: (jnp.maximum(i * per - 1, 0), 0))
    nxt = pl.BlockSpec((HALO, d), lambda i: (jnp.minimum((i + 1) * per, last), 0))
    return prev, nxt


def _normed_with_halo(xp_ref, x_ref, xn_ref, g_ref, sh_ref, sc_ref, tiles_per_seq):
    i = pl.program_id(0) % tiles_per_seq
    norm = lambda ref: _rms(ref[...], g_ref[...]) * (1.0 + sc_ref[0]) + sh_ref[0]
    keep_prev = jnp.where(i == 0, 0.0, 1.0)
    keep_next = jnp.where(i == tiles_per_seq - 1, 0.0, 1.0)
    return jnp.concatenate([norm(xp_ref) * keep_prev, norm(x_ref), norm(xn_ref) * keep_next], axis=0)


def _conv3(v, w_ref, tm):
    n, c = v.shape
    v3 = v.reshape(n // HALO, HALO, c)
    sub = lax.broadcasted_iota(jnp.int32, (1, HALO, 1), 1)
    down = pltpu.roll(v3, 1, 1)
    up = pltpu.roll(v3, HALO - 1, 1)
    k0, k1 = 1, 1 + tm // HALO
    prev = jnp.where(sub == 0, down[k0 - 1:k1 - 1], down[k0:k1])
    nxt = jnp.where(sub == HALO - 1, up[k0 + 1:k1 + 1], up[k0:k1])
    out = prev * w_ref[0:1, :][None] + v3[k0:k1] * w_ref[1:2, :][None] + nxt * w_ref[2:3, :][None]
    return out.reshape(tm, c)


SSD_CONV_TILE = 1024


def _ssd_inproj_kernel(xp_ref, x_ref, xn_ref, g_ref, sh_ref, sc_ref, w_ref, wdtt_ref, cw_ref, cb_ref, dtbt_ref,
                       z_ref, xs_ref, bcf_ref, bcb_ref, dt_ref, dtt_ref, *, tm, tiles_per_seq, inner):
    h_all = _normed_with_halo(xp_ref, x_ref, xn_ref, g_ref, sh_ref, sc_ref, tiles_per_seq)
    hb_all = h_all.astype(BF16)
    hb = h_all[HALO:HALO + tm].astype(BF16)

    z_ref[...] = _dot(hb, w_ref[:, 0:inner]).astype(BF16)
    nh2 = dtt_ref.shape[0]
    dtt = _softplus(_dot_nt(wdtt_ref[...], hb) + dtbt_ref[...])
    dtt_ref[...] = dtt[:nh2]
    dt_ref[...] = dtt.T[:, :nh2]

    cw = SSD_CONV_TILE
    for n0 in range(0, cw_ref.shape[1], cw):
        raw = _dot(hb_all, w_ref[:, inner + n0:inner + n0 + cw])
        act = _silu(_conv3(raw, cw_ref.at[:, n0:n0 + cw], tm) + cb_ref[:, n0:n0 + cw]).astype(BF16)
        nbc = bcf_ref.shape[1]
        if n0 < inner:
            xs_ref[:, n0:n0 + cw] = act
        elif n0 < inner + nbc:
            bcf_ref[:, n0 - inner:n0 - inner + cw] = act
        else:
            bcb_ref[:, n0 - inner - nbc:n0 - inner - nbc + cw] = act


def _ssd_inproj(x2, n_seq_rows, g0, shift, scale, tiles_per_group_rows, prm, tm):
    n_rows, d = x2.shape
    (w_in, layer), wdtt, conv_w, conv_b, dtbt, inner, nh2 = prm
    w_spec = pl.BlockSpec((None,) + w_in.shape[1:], lambda i: (layer, 0, 0), pipeline_mode=pl.Buffered(1))
    nbc = (conv_w.shape[1] - inner) // 2
    assert nbc % SSD_CONV_TILE == 0 and inner % SSD_CONV_TILE == 0
    x_spec, m_spec = _row_specs(tm, d, tiles_per_group_rows // tm)
    prev, nxt = _halo_specs(tm, d, n_rows)
    kern = functools.partial(_ssd_inproj_kernel, tm=tm, tiles_per_seq=n_seq_rows // tm, inner=inner)
    row_spec = lambda n: pl.BlockSpec((tm, n), lambda i: (i, 0))
    return pl.pallas_call(
        kern,
        grid=(n_rows // tm,),
        in_specs=[prev, x_spec, nxt, _full_spec((1, d)), m_spec, m_spec, w_spec,
                  _full_spec(wdtt.shape), _full_spec(conv_w.shape), _full_spec(conv_b.shape), _full_spec(dtbt.shape)],
        out_specs=[row_spec(inner), row_spec(inner), row_spec(nbc), row_spec(nbc), row_spec(nh2),
                   pl.BlockSpec((nh2, tm), lambda i: (0, i))],
        out_shape=[jax.ShapeDtypeStruct((n_rows, inner), BF16),
                   jax.ShapeDtypeStruct((n_rows, inner), BF16),
                   jax.ShapeDtypeStruct((n_rows, nbc), BF16),
                   jax.ShapeDtypeStruct((n_rows, nbc), BF16),
                   jax.ShapeDtypeStruct((n_rows, nh2), F32),
                   jax.ShapeDtypeStruct((nh2, n_rows), F32)],
        compiler_params=_cparams("parallel"),
        name="ssd_inproj",
    )(x2, x2, x2, g0, shift, scale, w_in, wdtt, conv_w, conv_b, dtbt)


def _ssd_chunk(d, r0, x_ref, bc_ref, dt_ref, dtt_ref, a_row, a_col, s_ref, y_ref, tri, mask, lane_lo):
    nh = a_row.shape[1] // 2
    hpg = nh // SSD_GROUPS
    rows = pl.ds(r0, SSD_CHUNK)
    dt = dt_ref[rows, :]
    dtt = dtt_ref[:, rows]
    cum = _dot(tri, dt * a_row, precision=HIGHEST) * LOG2_E
    cumt = _dot_nt(dtt * a_col, tri, precision=HIGHEST) * LOG2_E
    end = SSD_CHUNK - 1 if d == 0 else 0
    tot = cum[end:end + 1, :]
    tott = cumt[:, end:end + 1]
    wt = jnp.exp2(tott - cumt) * dtt
    dec_s = jnp.exp2(tot)
    cumt_dt = cumt - jnp.log2(dtt)
    for g in range(SSD_GROUPS):
        bg = bc_ref[rows, g * SSD_STATE:(g + 1) * SSD_STATE]
        cg = bc_ref[rows, (SSD_GROUPS + g) * SSD_STATE:(SSD_GROUPS + g + 1) * SSD_STATE]
        cb = _dot_nt(cg, bg)
        bt = bg.astype(F32).T
        cgf = cg.astype(F32)
        for p in range(hpg // 2):
            l0 = (g * hpg + 2 * p) * SSD_HEADDIM
            x2b = x_ref[rows, l0:l0 + 2 * SSD_HEADDIM]
            s2 = s_ref[0, d, :, l0:l0 + 2 * SSD_HEADDIM]
            rhs = jnp.concatenate([x2b, s2.astype(BF16)], axis=0)
            lhs_out, lhs_up, decs = [], [], []
            for hx in range(2):
                col = d * nh + g * hpg + 2 * p + hx
                colb = jnp.broadcast_to(cum[:, col:col + 1], (SSD_CHUNK, SSD_CHUNK))
                rowb = cumt_dt[col:col + 1, :]
                sc = (cb * jnp.exp2(jnp.where(mask, colb - rowb, NEG_BIG))).astype(BF16)
                co = (cgf * jnp.exp2(colb)).astype(BF16)
                lhs_out.append(jnp.concatenate([sc, co], axis=1))
                lhs_up.append((bt * wt[col:col + 1, :]).astype(BF16))
                decs.append(dec_s[:, col:col + 1])
            out2 = _dot(jnp.concatenate(lhs_out, axis=0), rhs)
            up2 = _dot(jnp.concatenate(lhs_up, axis=0), x2b)
            y_ref[rows, l0:l0 + 2 * SSD_HEADDIM] = jnp.where(
                lane_lo, out2[:SSD_CHUNK], out2[SSD_CHUNK:]).astype(BF16)
            s_ref[0, d, :, l0:l0 + 2 * SSD_HEADDIM] = (
                s2 * jnp.where(lane_lo, decs[0], decs[1])
                + jnp.where(lane_lo, up2[:SSD_STATE], up2[SSD_STATE:]))


def _ssd_scan_kernel(xf_ref, xb_ref, bcf_ref, bcb_ref, dtf_ref, dtb_ref, dttf_ref, dttb_ref,
                     al_ref, alt_ref, s0_ref, yf_ref, yb_ref, s_ref, *, n_chunk):
    @pl.when(pl.program_id(1) == 0)
    def _():
        s_ref[...] = s0_ref[...]

    a_row = -jnp.exp(al_ref[...])
    a_col = -jnp.exp(alt_ref[...])
    ii = lax.broadcasted_iota(jnp.int32, (SSD_CHUNK, SSD_CHUNK), 0)
    jj = lax.broadcasted_iota(jnp.int32, (SSD_CHUNK, SSD_CHUNK), 1)
    lane_lo = lax.broadcasted_iota(jnp.int32, (1, 2 * SSD_HEADDIM), 1) < SSD_HEADDIM
    for c in range(n_chunk):
        low = jj <= ii
        _ssd_chunk(0, c * SSD_CHUNK, xf_ref, bcf_ref, dtf_ref, dttf_ref, a_row, a_col, s_ref, yf_ref,
                   jnp.where(low, 1.0, 0.0), low, lane_lo)
        up = jj >= ii
        _ssd_chunk(1, (n_chunk - 1 - c) * SSD_CHUNK, xb_ref, bcb_ref, dtb_ref, dttb_ref, a_row, a_col, s_ref,
                   yb_ref, jnp.where(up, 1.0, 0.0), up, lane_lo)


def _ssd_scan(xs, bcf, bcb, dt, dtt, a_log_row, a_log_col, state0, batch, tb):
    n_rows, inner = xs.shape
    nbc = bcf.shape[1]
    nh2 = dt.shape[1]
    nb = n_rows // batch // tb
    fwd = lambda b, k: (b * nb + k, 0)
    bwd = lambda b, k: (b * nb + nb - 1 - k, 0)
    fwd_t = lambda b, k: (0, b * nb + k)
    bwd_t = lambda b, k: (0, b * nb + nb - 1 - k)
    s_spec = pl.BlockSpec((1, 2, SSD_STATE, inner), lambda b, k: (b, 0, 0, 0))
    kern = functools.partial(_ssd_scan_kernel, n_chunk=tb // SSD_CHUNK)
    return pl.pallas_call(
        kern,
        grid=(batch, nb),
        in_specs=[pl.BlockSpec((tb, inner), fwd), pl.BlockSpec((tb, inner), bwd),
                  pl.BlockSpec((tb, nbc), fwd), pl.BlockSpec((tb, nbc), bwd),
                  pl.BlockSpec((tb, nh2), fwd), pl.BlockSpec((tb, nh2), bwd),
                  pl.BlockSpec((nh2, tb), fwd_t), pl.BlockSpec((nh2, tb), bwd_t),
                  _full_spec(a_log_row.shape), _full_spec(a_log_col.shape), s_spec],
        out_specs=[pl.BlockSpec((tb, inner), fwd), pl.BlockSpec((tb, inner), bwd), s_spec],
        out_shape=[jax.ShapeDtypeStruct((n_rows, inner), BF16),
                   jax.ShapeDtypeStruct((n_rows, inner), BF16),
                   jax.ShapeDtypeStruct(state0.shape, F32)],
        compiler_params=_cparams("parallel", "arbitrary"),
        name="ssd_scan",
    )(xs, xs, bcf, bcb, dt, dt, dtt, dtt, a_log_row, a_log_col, state0)


def _ssd_gate(yf_ref, yb_ref, xs_ref, z_ref, d_ref, g_ref):
    y = yf_ref[...].astype(F32) + yb_ref[...].astype(F32) + d_ref[...] * xs_ref[...].astype(F32)
    y = y * _silu(z_ref[...].astype(F32))
    return _rms(y, g_ref[...]).astype(BF16)


def _qkv_kernel(x_ref, g_ref, sh_ref, sc_ref, w_ref, q_ref, k_ref, v_ref):
    d = x_ref.shape[1]
    for r0 in range(0, x_ref.shape[0], FINISH_SUB):
        rows = pl.ds(r0, FINISH_SUB)
        hb = (_rms(x_ref[rows, :], g_ref[...]) * (1.0 + sc_ref[0]) + sh_ref[0]).astype(BF16)
        q_ref[rows, :] = (_dot(hb, w_ref[:, 0:d]) * (NA_HEADDIM ** -0.5 * LOG2_E)).astype(BF16)
        k_ref[rows, :] = _dot(hb, w_ref[:, d:2 * d]).astype(BF16)
        v_ref[rows, :] = _dot(hb, w_ref[:, 2 * d:3 * d]).astype(BF16)


def _qkv(x2, g0, shift, scale, group_rows, w, tm):
    n_rows, d = x2.shape
    x_spec, m_spec = _row_specs(tm, d, group_rows // tm)
    out = jax.ShapeDtypeStruct((n_rows, d), BF16)
    return pl.pallas_call(
        _qkv_kernel,
        grid=(n_rows // tm,),
        in_specs=[x_spec, _full_spec((1, d)), m_spec, m_spec, _full_spec(w.shape)],
        out_specs=[x_spec, x_spec, x_spec],
        out_shape=[out, out, out],
        compiler_params=_cparams("parallel"),
        name="na_qkv",
    )(x2, g0, shift, scale, w)


NA_STEP_ROWS = 2
NA_SPAN_ROWS = NA_WIN_ROWS + 2


def _na_span_start(step, rows):
    return jnp.clip(step * NA_STEP_ROWS - NA_WIN_ROWS // 2, 0, rows - NA_SPAN_ROWS)


def _na_kernel(q_ref, k_ref, v_ref, kc_ref, vc_ref, bias_ref, o_ref):
    step = pl.program_id(1)
    rows = pl.num_programs(1) * NA_STEP_ROWS
    base = _na_span_start(step, rows)
    lane_lo = lax.broadcasted_iota(jnp.int32, (1, 2 * NA_HEADDIM), 1) < NA_HEADDIM
    pair_lanes = [slice(p * 2 * NA_HEADDIM, (p + 1) * 2 * NA_HEADDIM) for p in range(NA_HEADS // 2)]
    n_pairs = len(pair_lanes)
    nq = GRID_W

    def row_params(par):
        r = step * NA_STEP_ROWS + par
        r0 = jnp.clip(r - NA_WIN_ROWS // 2, 0, rows - NA_WIN_ROWS)
        win = pl.ds(pl.multiple_of((r0 - base) * GRID_W, GRID_W), NA_WIN_ROWS * GRID_W)
        return pl.ds(par * GRID_W, GRID_W), win, r0 - r + NA_WIN_ROWS - 1

    def qk(p, q_rows, win):
        lanes = pair_lanes[p]
        q2 = q_ref[q_rows, lanes]
        zero = jnp.zeros_like(q2)
        qs = jnp.concatenate([jnp.where(lane_lo, q2, zero), jnp.where(lane_lo, zero, q2)], axis=0)
        return _dot_nt(qs, k_ref[win, lanes]), _dot_nt(qs, kc_ref[0, :, lanes])

    def softmax(p, ro0, s_loc, s_ctx):
        bias = jnp.concatenate(
            [jnp.concatenate([bias_ref[2 * p + hx, ro0 + 2 * w2] for w2 in range(NA_WIN_ROWS // 2)], axis=1)
             for hx in range(2)], axis=0)
        s_loc = s_loc + bias
        m = jnp.maximum(jnp.max(s_loc, axis=-1, keepdims=True), jnp.max(s_ctx, axis=-1, keepdims=True))
        p_loc = jnp.exp2(s_loc - m)
        p_ctx = jnp.exp2(s_ctx - m)
        den = jnp.sum(p_loc, axis=-1, keepdims=True) + jnp.sum(p_ctx, axis=-1, keepdims=True)
        return p_loc.astype(BF16), p_ctx.astype(BF16), den

    def pv(p, q_rows, win, p_loc, p_ctx, den):
        lanes = pair_lanes[p]
        o = (_dot(p_loc, v_ref[win, lanes]) + _dot(p_ctx, vc_ref[0, :, lanes])) / den
        o_ref[q_rows, lanes] = jnp.where(lane_lo, o[:nq], o[nq:]).astype(BF16)

    params = [row_params(par) for par in range(NA_STEP_ROWS)]
    scores = [[qk(p, q_rows, win) for p in range(n_pairs)] for q_rows, win, _ in params]
    for (q_rows, win, ro0), row_scores in zip(params, scores):
        probs = [softmax(p, ro0, *s) for p, s in enumerate(row_scores)]
        for p, pr in enumerate(probs):
            pv(p, q_rows, win, *pr)


def _bias_expand_kernel(rpb_ref, sel_ref, o_ref):
    o_ref[...] = _dot(rpb_ref[...], sel_ref[...], precision=HIGHEST)


def _na_bias_table(rpb):
    n_heads, n_ro, n_co = rpb.shape
    cols = np.arange(GRID_W)
    c0 = np.clip(cols - NA_WIN_COLS // 2, 0, GRID_W - NA_WIN_COLS)
    valid = (cols[None, :] >= c0[:, None]) & (cols[None, :] < c0[:, None] + NA_WIN_COLS)
    co = cols[None, :] - cols[:, None] + NA_WIN_COLS - 1
    sel = np.zeros((n_co + 1, GRID_W, GRID_W), np.float32)
    qi, ki = np.nonzero(valid)
    sel[co[qi, ki], qi, ki] = 1.0
    sel[n_co][~valid] = NEG_BIG
    sel = jnp.asarray(sel.reshape(n_co + 1, GRID_W * GRID_W))
    rows = jnp.concatenate([rpb.reshape(n_heads * n_ro, n_co) * LOG2_E, jnp.ones((n_heads * n_ro, 1), F32)],
                           axis=1)
    toe = pl.pallas_call(
        _bias_expand_kernel,
        out_shape=jax.ShapeDtypeStruct((n_heads * n_ro, GRID_W * GRID_W), F32),
        name="na_bias_expand",
    )(rows, sel).reshape(n_heads, n_ro, GRID_W, GRID_W)
    return jnp.concatenate([toe[:, :-1], toe[:, 1:]], axis=-1)


def _na_attention(q, k, v, kc, vc, bias_tab, batch):
    n_rows, d = q.shape
    rows = n_rows // batch // GRID_W
    steps = rows // NA_STEP_ROWS
    assert rows % NA_STEP_ROWS == 0 and rows >= NA_SPAN_ROWS
    q_spec = pl.BlockSpec((NA_STEP_ROWS * GRID_W, d), lambda b, s: (b * steps + s, 0))
    win = pl.BlockSpec((pl.Element(NA_SPAN_ROWS * GRID_W), pl.Element(d)),
                       lambda b, s: ((b * rows + _na_span_start(s, rows)) * GRID_W, 0))
    c_spec = pl.BlockSpec((1,) + kc.shape[1:], lambda b, s: (b, 0, 0))
    return pl.pallas_call(
        _na_kernel,
        grid=(batch, steps),
        in_specs=[q_spec, win, win, c_spec, c_spec, _full_spec(bias_tab.shape)],
        out_specs=q_spec,
        out_shape=jax.ShapeDtypeStruct((n_rows, d), BF16),
        compiler_params=_cparams("parallel", "arbitrary"),
        name="na_attention",
    )(q, k, v, kc, vc, bias_tab)


def _sconv_kernel(xp_ref, x_ref, xn_ref, g_ref, sh_ref, sc_ref, w_ref, cw_ref, o_ref, *, tm, tiles_per_seq):
    d = x_ref.shape[1]
    h_all = _normed_with_halo(xp_ref, x_ref, xn_ref, g_ref, sh_ref, sc_ref, tiles_per_seq)
    hb_all = h_all.astype(BF16)
    hb = h_all[HALO:HALO + tm].astype(BF16)
    cw = 512
    for n0 in range(0, d, cw):
        c_gate = _dot(hb_all, w_ref[:, d + n0:d + n0 + cw])
        u = _dot(hb_all, w_ref[:, 2 * d + n0:2 * d + n0 + cw])
        conv = _conv3(c_gate * u, cw_ref.at[:, n0:n0 + cw], tm)
        b_gate = _dot(hb, w_ref[:, n0:n0 + cw])
        o_ref[:, n0:n0 + cw] = (b_gate * conv).astype(BF16)


def _sconv(x2, seq_rows, g0, shift, scale, w, conv_w, tm):
    n_rows, d = x2.shape
    x_spec, m_spec = _row_specs(tm, d, seq_rows // tm)
    prev, nxt = _halo_specs(tm, d, n_rows)
    kern = functools.partial(_sconv_kernel, tm=tm, tiles_per_seq=seq_rows // tm)
    return pl.pallas_call(
        kern,
        grid=(n_rows // tm,),
        in_specs=[prev, x_spec, nxt, _full_spec((1, d)), m_spec, m_spec, _full_spec(w.shape),
                  _full_spec(conv_w.shape)],
        out_specs=x_spec,
        out_shape=jax.ShapeDtypeStruct((n_rows, d), BF16),
        compiler_params=_cparams("parallel"),
        name="sconv",
    )(x2, x2, x2, g0, shift, scale, w, conv_w)


def _fourier1_kernel(x_ref, g_ref, sh_ref, sc_ref, cs_ref, km_ref, twc_ref, tws_ref, yr_ref, yi_ref):
    d = g_ref.shape[1]
    gc = d // FN_GROUPS
    n = DFT_ROWS * FN_SUB
    x = x_ref[0].reshape(n, d)
    hb = (_rms(x, g_ref[...]) * (1.0 + sc_ref[0]) + sh_ref[0]).astype(BF16)
    zr, zi = [], []
    for g in range(FN_GROUPS):
        zz = _dot(hb[:, g * gc:(g + 1) * gc], cs_ref[...])
        zr.append(zz[:, :gc])
        zi.append(zz[:, gc:])
    z = jnp.concatenate([jnp.concatenate(zr, axis=1), jnp.concatenate(zi, axis=1)], axis=0)
    y = _dot(km_ref[...], z.astype(BF16))
    yr, yi = y[:n], y[n:]
    tc, ts = twc_ref[0], tws_ref[0]
    yr_ref[0] = (yr * tc + yi * ts).reshape(FN_SUB, DFT_ROWS, d)
    yi_ref[0] = (yi * tc - yr * ts).reshape(FN_SUB, DFT_ROWS, d)


def _fourier2_kernel(yr_ref, yi_ref, gk_ref, o_ref):
    t2n, _, d = yr_ref.shape[1:]
    n = t2n * FN_SUB
    y = jnp.concatenate([yr_ref[0].reshape(n, d), yi_ref[0].reshape(n, d)], axis=0)
    o_ref[0] = _dot(gk_ref[...], y.astype(BF16)).reshape(t2n, FN_SUB, d)


def _dft_tables(seq, gc):
    t2n = seq // DFT_ROWS
    eye = np.eye(FN_SUB)
    c = np.arange(gc)
    ang_c = 2.0 * np.pi * np.outer(c, c) / gc
    cs = np.concatenate([np.cos(ang_c), -np.sin(ang_c)], axis=1)
    k1 = np.arange(DFT_ROWS)
    ang1 = 2.0 * np.pi * np.outer(k1, k1) / DFT_ROWS
    c1, s1 = np.cos(ang1), np.sin(ang1)
    m = np.stack([np.stack([c1, s1], axis=1), np.stack([-s1, c1], axis=1)], axis=0)
    km = np.einsum("akbt,sr->askbtr", m, eye).reshape(2 * DFT_ROWS * FN_SUB, 2 * DFT_ROWS * FN_SUB)
    t2 = np.arange(t2n)
    ang_tw = 2.0 * np.pi * np.outer(t2, k1) / seq
    tw_shape = (t2n // FN_SUB, FN_SUB * DFT_ROWS, 1)
    ang2 = 2.0 * np.pi * np.outer(t2, t2) / t2n
    g2 = np.stack([np.cos(ang2), np.sin(ang2)], axis=1)
    gk = np.einsum("kbt,sr->ksbtr", g2, eye).reshape(t2n * FN_SUB, 2 * t2n * FN_SUB)
    return (jnp.asarray(cs, BF16), jnp.asarray(km, BF16),
            jnp.asarray(np.cos(ang_tw).reshape(tw_shape), F32), jnp.asarray(np.sin(ang_tw).reshape(tw_shape), F32),
            jnp.asarray(gk, BF16))


def _fourier(x3, g0, shift, scale):
    batch, seq, d = x3.shape
    t2n = seq // DFT_ROWS
    n1 = DFT_ROWS * FN_SUB
    cs, km, twc, tws, gk = _dft_tables(seq, d // FN_GROUPS)
    xv = x3.reshape(batch, DFT_ROWS, t2n, d)
    m_spec = pl.BlockSpec((1, 1, d), lambda b, j: (b, 0, 0))
    y1_spec = pl.BlockSpec((1, FN_SUB, DFT_ROWS, d), lambda b, j: (b, j, 0, 0))
    tw_spec = pl.BlockSpec((1, n1, 1), lambda b, j: (j, 0, 0))
    y_shape = jax.ShapeDtypeStruct((batch, t2n, DFT_ROWS, d), F32)
    yr, yi = pl.pallas_call(
        _fourier1_kernel,
        grid=(batch, t2n // FN_SUB),
        in_specs=[pl.BlockSpec((1, DFT_ROWS, FN_SUB, d), lambda b, j: (b, 0, j, 0)),
                  _full_spec((1, d)), m_spec, m_spec, _full_spec(cs.shape), _full_spec(km.shape), tw_spec, tw_spec],
        out_specs=[y1_spec, y1_spec],
        out_shape=[y_shape, y_shape],
        compiler_params=_cparams("parallel", "parallel"),
        name="fourier_stage1",
    )(xv, g0, shift, scale, cs, km, twc, tws)
    y2_spec = pl.BlockSpec((1, t2n, FN_SUB, d), lambda b, j: (b, 0, j, 0))
    out = pl.pallas_call(
        _fourier2_kernel,
        grid=(batch, DFT_ROWS // FN_SUB),
        in_specs=[y2_spec, y2_spec, _full_spec(gk.shape)],
        out_specs=y2_spec,
        out_shape=y_shape,
        compiler_params=_cparams("parallel", "parallel"),
        name="fourier_stage2",
    )(yr, yi, gk)
    return out.reshape(batch * seq, d)


def _finish_kernel(x_ref, *refs, hc, n_mixer_in):
    mixer_in = refs[:n_mixer_in]
    wo_ref, m2_ref, m3_ref, m4_ref, m5_ref, g_ref, w1_ref, w2_ref, o_ref = refs[n_mixer_in:]
    subs = [pl.ds(r0, FINISH_SUB) for r0 in range(0, x_ref.shape[0], FINISH_SUB)]

    def out_proj(rows):
        if n_mixer_in > 1:
            z = _ssd_gate(*[ref.at[rows, :] for ref in mixer_in[:4]], *mixer_in[4:])
        else:
            z = mixer_in[0][rows, :].astype(BF16)
        return _dot(z, wo_ref[...])

    def mlp(rows, y):
        t = x_ref[rows, :] + m2_ref[0] * _rms(y, g_ref[1:2, :])
        h2 = (_rms(t, g_ref[2:3, :]) * (1.0 + m4_ref[0]) + m3_ref[0]).astype(BF16)
        acc = None
        for n0 in range(0, w1_ref.shape[1], hc):
            a = jnp.maximum(_dot(h2, w1_ref[:, n0:n0 + hc]), 0.0)
            down = _dot((a * a).astype(BF16), w2_ref[n0:n0 + hc, :])
            acc = down if acc is None else acc + down
        o_ref[rows, :] = t + m5_ref[0] * _rms(acc, g_ref[3:4, :])

    ys = [out_proj(rows) for rows in subs]
    for rows, y in zip(subs, ys):
        mlp(rows, y)


def _finish(x2, mixer_out, w_out, mods, g, w1, w2, group_rows, tm):
    n_rows, d = x2.shape
    x_spec, m_spec = _row_specs(tm, d, group_rows // tm)
    row_spec = lambda a: pl.BlockSpec((tm, a.shape[1]), lambda i: (i, 0))
    if isinstance(mixer_out, tuple):
        mixer_specs = [row_spec(a) for a in mixer_out[:4]] + [_full_spec(a.shape) for a in mixer_out[4:]]
    else:
        mixer_out = (mixer_out,)
        mixer_specs = [row_spec(mixer_out[0])]
    kern = functools.partial(_finish_kernel, hc=1024, n_mixer_in=len(mixer_out))
    (w1_all, layer), (w2_all, _) = w1, w2
    layer_spec = lambda a: pl.BlockSpec((None,) + a.shape[1:], lambda i: (layer, 0, 0),
                                        pipeline_mode=pl.Buffered(1))
    return pl.pallas_call(
        kern,
        grid=(n_rows // tm,),
        in_specs=[x_spec] + mixer_specs + [_full_spec(w_out.shape), m_spec, m_spec, m_spec, m_spec,
                                           _full_spec(g.shape), layer_spec(w1_all), layer_spec(w2_all)],
        out_specs=x_spec,
        out_shape=jax.ShapeDtypeStruct((n_rows, d), F32),
        compiler_params=_cparams("parallel"),
        name="finish",
    )(x2, *mixer_out, w_out, mods[2], mods[3], mods[4], mods[5], g, w1_all, w2_all)


def _split_mods(m_rows, d):
    return [m_rows[:, k * d:(k + 1) * d][:, None, :] for k in range(N_MOD)]


def kernel(x, c, ctx, c_ctx, mod_w, mod_b, norm_g, mlp_w1, mlp_w2, ssd_w_in, ssd_conv_w, ssd_conv_b, ssd_dt_bias,
           ssd_a_log, ssd_d, ssd_norm_g, ssd_w_out, na_w_qkv, na_rpb, na_w_out, sc_w_in, sc_conv_w, sc_w_out,
           fn_w_out):
    batch, seq, d = x.shape
    ctx_len = ctx.shape[1]
    depth = mod_w.shape[0]
    n_lat = batch * seq
    n_ctx = batch * ctx_len
    tm = min(256, ctx_len)

    c_rows = jnp.concatenate([c, c_ctx[None, :], jnp.zeros((8 - batch - 1, d), F32)], axis=0)
    mods_all = _modulation(c_rows, mod_w, mod_b)

    xl = x.reshape(n_lat, d)
    xc = ctx.reshape(n_ctx, d)
    mlp_w1_bf = mlp_w1.astype(BF16)
    mlp_w2_bf = mlp_w2.astype(BF16)

    for i in range(depth):
        kind, j = i % 4, i // 4
        m_lat = _split_mods(mods_all[i, :batch], d)
        m_ctx = _split_mods(mods_all[i, batch:batch + 1], d)
        g = norm_g[i]
        g0 = g[0:1]
        w1, w2 = (mlp_w1_bf, i), (mlp_w2_bf, i)
        ctx_later = any((l % 4) in (0, 1) for l in range(i + 1, depth))
        zc = None
        if kind == 0:
            inner = ssd_w_out.shape[1]
            nh = ssd_a_log.shape[2]
            w_in = (ssd_w_in.astype(BF16), j)
            n_conv = ssd_conv_w.shape[2]
            pad_rows = 128 - 2 * nh
            w_dtt = jnp.pad(ssd_w_in[j][:, inner + n_conv:].T.astype(BF16), ((0, pad_rows), (0, 0)))
            dt_bias_col = jnp.pad(ssd_dt_bias[j].reshape(2 * nh, 1), ((0, pad_rows), (0, 0)))
            prm = (w_in, w_dtt, ssd_conv_w[j], ssd_conv_b[j][None, :], dt_bias_col, inner, 2 * nh)
            a_row = ssd_a_log[j].reshape(1, 2 * nh)
            a_col = ssd_a_log[j].reshape(2 * nh, 1)
            d_vec = jnp.repeat(ssd_d[j].sum(0), SSD_HEADDIM)[None, :]
            ng = ssd_norm_g[j][None, :]
            w_out = ssd_w_out[j].astype(BF16)
            tb = min(256, ctx_len)

            def mixer(rows2, seq_rows, group_rows, shift, scale, state0):
                z, xs, bcf, bcb, dt, dtt = _ssd_inproj(rows2, seq_rows, g0, shift, scale, group_rows, prm, tm)
                yf, yb, s_fin = _ssd_scan(xs, bcf, bcb, dt, dtt, a_row, a_col, state0, batch, min(2 * tb, seq_rows))
                return (yf, yb, xs, z, d_vec, ng), s_fin

            zero_state = jnp.zeros((batch, 2, SSD_STATE, inner), F32)
            zc, ctx_state = mixer(xc, ctx_len, n_ctx, m_ctx[0], m_ctx[1], zero_state)
            zl, _ = mixer(xl, seq, seq, m_lat[0], m_lat[1], ctx_state)
        elif kind == 1:
            w_qkv = na_w_qkv[j].astype(BF16)
            w_out = na_w_out[j].astype(BF16)
            qc, kc, vc = _qkv(xc, g0, m_ctx[0], m_ctx[1], n_ctx, w_qkv, min(2 * FINISH_SUB, n_ctx))
            q, k, v = _qkv(xl, g0, m_lat[0], m_lat[1], seq, w_qkv, min(4 * FINISH_SUB, seq))
            kc = kc.reshape(batch, ctx_len, d)
            vc = vc.reshape(batch, ctx_len, d)
            zl = _na_attention(q, k, v, kc, vc, _na_bias_table(na_rpb[j]), batch)
            if ctx_later:
                raise NotImplementedError("context self-attention is only needed when a later layer reads ctx")
        elif kind == 2:
            w_out = sc_w_out[j].astype(BF16)
            zl = _sconv(xl, seq, g0, m_lat[0], m_lat[1], sc_w_in[j].astype(BF16), sc_conv_w[j], min(4 * tm, seq))
            if ctx_later:
                zc = _sconv(xc, ctx_len, g0, m_ctx[0], m_ctx[1], sc_w_in[j].astype(BF16), sc_conv_w[j], tm)
        else:
            w_out = fn_w_out[j].astype(BF16)
            zl = _fourier(xl.reshape(batch, seq, d), g0, m_lat[0], m_lat[1])
            if ctx_later:
                raise NotImplementedError("context Fourier mixing is only needed when a later layer reads ctx")
        tm_fin = min((2 if kind == 0 else 4) * FINISH_SUB, seq)
        xl = _finish(xl, zl, w_out, m_lat, g, w1, w2, seq, tm_fin)
        if ctx_later:
            xc = _finish(xc, zc, w_out, m_ctx, g, w1, w2, n_ctx, min(tm_fin, n_ctx))
    return xl.reshape(batch, seq, d)
```
